```python
import math
import jax, jax.numpy as jnp
from jax import lax
import numpy as np

D_MODEL = 4096
BATCH = 2
SEQ = 4096
DEPTH = 4

N_MIXERS = 2
ROPE_THETA = 500000.0
EPS = 1e-6
Q_BLOCK = 128
D_FF = 4 * D_MODEL
PLE_DIM = 256

DIFF_HEAD_DIM = 128
DIFF_HEADS = D_MODEL // (2 * DIFF_HEAD_DIM)
DIFF_ROT = DIFF_HEAD_DIM // 4
DIFF_QK_WIDTH = DIFF_HEADS * 2 * DIFF_HEAD_DIM
DIFF_V_WIDTH = DIFF_HEADS * 2 * DIFF_HEAD_DIM
DIFF_IN_WIDTH = 2 * DIFF_QK_WIDTH + DIFF_V_WIDTH

MLA_HEADS = 32
MLA_NOPE = 128
MLA_ROPE = 64
MLA_V = 128
MLA_QK = MLA_NOPE + MLA_ROPE
MLA_Q_RANK = 1024
MLA_KV_RANK = 512
MLA_IN_WIDTH = MLA_Q_RANK + MLA_KV_RANK + MLA_ROPE

N_DIFF_LAYERS = (DEPTH + 1) // 2
N_MLA_LAYERS = DEPTH // 2

kernel_name = "hybrid_diffattn_mla_encoder"


def rms_norm(x, g):
    x32 = x.astype(jnp.float32)
    y = x32 * lax.rsqrt(jnp.mean(x32 * x32, axis=-1, keepdims=True) + EPS)
    return (y * g.astype(jnp.float32)).astype(x.dtype)


def rope_tables(positions, rot_dim):
    inv = ROPE_THETA ** (-jnp.arange(0, rot_dim, 2, dtype=jnp.float32) / rot_dim)
    ang = positions.astype(jnp.float32)[..., None] * inv
    return jnp.cos(ang), jnp.sin(ang)


def apply_rope(x, cos, sin):
    x32 = x.astype(jnp.float32)
    x1, x2 = jnp.split(x32, 2, axis=-1)
    out = jnp.concatenate([x1 * cos - x2 * sin, x2 * cos + x1 * sin], axis=-1)
    return out.astype(x.dtype)


def to_blocks(t):
    b, s = t.shape[0], t.shape[1]
    t = t.reshape((b, s // Q_BLOCK, Q_BLOCK) + t.shape[2:])
    return jnp.moveaxis(t, 1, 0)


def from_blocks(t):
    t = jnp.moveaxis(t, 0, 1)
    return t.reshape((t.shape[0], t.shape[1] * t.shape[2]) + t.shape[3:])


def diff_attention(h, w_in, w_out, g_q, g_k, lam, g_sub, cos, sin, lambda_init):
    b, s, _ = h.shape
    z = h @ w_in
    q, k, v = jnp.split(z, [DIFF_QK_WIDTH, 2 * DIFF_QK_WIDTH], axis=-1)
    q = q.reshape(b, s, DIFF_HEADS, 2, DIFF_HEAD_DIM)
    k = k.reshape(b, s, DIFF_HEADS, 2, DIFF_HEAD_DIM)
    v = v.reshape(b, s, DIFF_HEADS, 2 * DIFF_HEAD_DIM)
    c5 = cos[:, :, None, None, :]
    s5 = sin[:, :, None, None, :]
    q = rms_norm(q, g_q)
    k = rms_norm(k, g_k)
    q = jnp.concatenate([apply_rope(q[..., :DIFF_ROT], c5, s5), q[..., DIFF_ROT:]], axis=-1)
    k = jnp.concatenate([apply_rope(k[..., :DIFF_ROT], c5, s5), k[..., DIFF_ROT:]], axis=-1)
    lam32 = lam.astype(jnp.float32)
    lam_full = (jnp.exp(jnp.sum(lam32[0] * lam32[1]))
                - jnp.exp(jnp.sum(lam32[2] * lam32[3])) + lambda_init)
    scale = DIFF_HEAD_DIM ** -0.5

    def block(qb):
        sc = jnp.einsum('bqhcd,bkhcd->bhcqk', qb, k).astype(jnp.float32) * scale
        pr = jax.nn.softmax(sc, axis=-1)
        a = pr[:, :, 0] - lam_full * pr[:, :, 1]
        return jnp.einsum('bhqk,bkhe->bqhe', a.astype(v.dtype), v)

    o = from_blocks(lax.map(block, to_blocks(q)))
    o = rms_norm(o, g_sub) * (1.0 - lambda_init)
    return o.reshape(b, s, DIFF_V_WIDTH) @ w_out


def mla_attention(h, w_in, g_cq, g_ckv, w_uq, w_ukv, g_q, g_k, w_out, cos, sin):
    b, s, _ = h.shape
    z = h @ w_in
    c_q, c_kv, k_rope = jnp.split(z, [MLA_Q_RANK, MLA_Q_RANK + MLA_KV_RANK], axis=-1)
    q = (rms_norm(c_q, g_cq) @ w_uq).reshape(b, s, MLA_HEADS, MLA_QK)
    kv = (rms_norm(c_kv, g_ckv) @ w_ukv).reshape(b, s, MLA_HEADS, MLA_NOPE + MLA_V)
    k_nope, v = jnp.split(kv, [MLA_NOPE], axis=-1)
    q_nope = rms_norm(q[..., :MLA_NOPE], g_q[:MLA_NOPE])
    q_rope = rms_norm(q[..., MLA_NOPE:], g_q[MLA_NOPE:])
    k_nope = rms_norm(k_nope, g_k[:MLA_NOPE])
    k_rope = rms_norm(k_rope, g_k[MLA_NOPE:])
    q_rope = apply_rope(q_rope, cos[:, :, None, :], sin[:, :, None, :])
    k_rope = apply_rope(k_rope, cos, sin)
    q = jnp.concatenate([q_nope, q_rope], axis=-1)
    k = jnp.concatenate(
        [k_nope, jnp.broadcast_to(k_rope[:, :, None, :], (b, s, MLA_HEADS, MLA_ROPE))], axis=-1)
    scale = MLA_QK ** -0.5

    def block(qb):
        sc = jnp.einsum('bqhd,bkhd->bhqk', qb, k).astype(jnp.float32) * scale
        pr = jax.nn.softmax(sc, axis=-1).astype(v.dtype)
        return jnp.einsum('bhqk,bkhe->bqhe', pr, v)

    o = from_blocks(lax.map(block, to_blocks(q)))
    return o.reshape(b, s, MLA_HEADS * MLA_V) @ w_out


def sq_relu_mlp(h, w1, w2):
    a = jax.nn.relu(h @ w1)
    return (a * a) @ w2


def setup_inputs(seed: int = 0) -> dict:
    key = jax.random.key(seed)
    ks = jax.random.split(key, 24)
    f32 = jnp.float32

    def nrm(k, shape, fan_in):
        return jax.random.normal(k, shape, f32) * (fan_in ** -0.5)

    def gain(k, shape):
        return 1.0 + 0.01 * jax.random.normal(k, shape, f32)

    ND, NM = N_DIFF_LAYERS, N_MLA_LAYERS
    x = jax.random.normal(ks[0], (BATCH, SEQ, D_MODEL), f32)
    p = jax.random.normal(ks[1], (DEPTH, BATCH, SEQ, PLE_DIM), f32)
    positions = jnp.broadcast_to(jnp.arange(SEQ, dtype=jnp.int32), (BATCH, SEQ))
    return {
        "x": x,
        "p": p,
        "positions": positions,
        "g_mix": gain(ks[2], (DEPTH, D_MODEL)),
        "g_mlp": gain(ks[3], (DEPTH, D_MODEL)),
        "g_ple": gain(ks[4], (DEPTH, D_MODEL)),
        "w1": nrm(ks[5], (DEPTH, D_MODEL, D_FF), D_MODEL),
        "w2": nrm(ks[6], (DEPTH, D_FF, D_MODEL), D_FF),
        "w_gate": nrm(ks[7], (DEPTH, D_MODEL, D_MODEL), D_MODEL),
        "w_ple": nrm(ks[8], (DEPTH, PLE_DIM, D_MODEL), PLE_DIM),
        "diff_w_in": nrm(ks[9], (ND, D_MODEL, DIFF_IN_WIDTH), D_MODEL),
        "diff_w_out": nrm(ks[10], (ND, DIFF_V_WIDTH, D_MODEL), DIFF_V_WIDTH),
        "diff_g_q": gain(ks[11], (ND, DIFF_HEAD_DIM)),
        "diff_g_k": gain(ks[12], (ND, DIFF_HEAD_DIM)),
        "diff_lambda": 0.1 * jax.random.normal(ks[13], (ND, 4, DIFF_HEAD_DIM), f32),
        "diff_g_sub": gain(ks[14], (ND, 2 * DIFF_HEAD_DIM)),
        "mla_w_in": nrm(ks[15], (NM, D_MODEL, MLA_IN_WIDTH), D_MODEL),
        "mla_g_cq": gain(ks[16], (NM, MLA_Q_RANK)),
        "mla_g_ckv": gain(ks[17], (NM, MLA_KV_RANK)),
        "mla_w_uq": nrm(ks[18], (NM, MLA_Q_RANK, MLA_HEADS * MLA_QK), MLA_Q_RANK),
        "mla_w_ukv": nrm(ks[19], (NM, MLA_KV_RANK, MLA_HEADS * (MLA_NOPE + MLA_V)), MLA_KV_RANK),
        "mla_g_q": gain(ks[20], (NM, MLA_QK)),
        "mla_g_k": gain(ks[21], (NM, MLA_QK)),
        "mla_w_out": nrm(ks[22], (NM, MLA_HEADS * MLA_V, D_MODEL), MLA_HEADS * MLA_V),
    }


def reference(x, p, positions, g_mix, g_mlp, g_ple, w1, w2, w_gate, w_ple,
              diff_w_in, diff_w_out, diff_g_q, diff_g_k, diff_lambda, diff_g_sub,
              mla_w_in, mla_g_cq, mla_g_ckv, mla_w_uq, mla_w_ukv, mla_g_q, mla_g_k, mla_w_out):
    cos_d, sin_d = rope_tables(positions, DIFF_ROT)
    cos_m, sin_m = rope_tables(positions, MLA_ROPE)
    for i in range(DEPTH):
        j = i // N_MIXERS
        h = rms_norm(x, g_mix[i])
        if i % N_MIXERS == 0:
            lambda_init = 0.8 - 0.6 * math.exp(-0.3 * i)
            mix = diff_attention(h, diff_w_in[j], diff_w_out[j], diff_g_q[j], diff_g_k[j],
                                 diff_lambda[j], diff_g_sub[j], cos_d, sin_d, lambda_init)
        else:
            mix = mla_attention(h, mla_w_in[j], mla_g_cq[j], mla_g_ckv[j], mla_w_uq[j],
                                mla_w_ukv[j], mla_g_q[j], mla_g_k[j], mla_w_out[j], cos_m, sin_m)
        x = x + mix
        x = x + sq_relu_mlp(rms_norm(x, g_mlp[i]), w1[i], w2[i])
        gate = jax.nn.sigmoid(rms_norm(x, g_ple[i]) @ w_gate[i])
        x = x + gate * (p[i] @ w_ple[i])
    return x
```

```python
import functools
import math

import jax
import jax.numpy as jnp
from jax import lax
from jax.experimental import pallas as pl
from jax.experimental.pallas import tpu as pltpu

D_MODEL = 4096
DEPTH = 4
N_MIXERS = 2
ROPE_THETA = 500000.0
EPS = 1e-6
D_FF = 4 * D_MODEL
PLE_DIM = 256

DIFF_HEAD_DIM = 128
DIFF_HEADS = D_MODEL // (2 * DIFF_HEAD_DIM)
DIFF_ROT = DIFF_HEAD_DIM // 4
DIFF_QK_WIDTH = DIFF_HEADS * 2 * DIFF_HEAD_DIM
DIFF_V_WIDTH = DIFF_HEADS * 2 * DIFF_HEAD_DIM

MLA_HEADS = 32
MLA_NOPE = 128
MLA_ROPE = 64
MLA_V = 128
MLA_QK = MLA_NOPE + MLA_ROPE
MLA_Q_RANK = 1024
MLA_KV_RANK = 512

LANES = 128
MLA_QK_PAD = 2 * LANES
V7X_VMEM_BYTES = 64 * 1024 * 1024
VMEM_TEMP_ALLOWANCE = 12 * 1024 * 1024

F32 = jnp.float32
BF16 = jnp.bfloat16


def _nbytes(shape, dtype):
    return math.prod(shape) * jnp.dtype(dtype).itemsize


def _vmem_limit(blocks, scratch=()):
    need = 2 * sum(_nbytes(s, d) for s, d in blocks) + sum(_nbytes(s, d) for s, d in scratch)
    return min(need + VMEM_TEMP_ALLOWANCE, V7X_VMEM_BYTES - 4 * 1024 * 1024)


def _rope_table_kernel(pos_ref, inv_ref, c_ref, s1_ref, s2_ref, *, half, keep_rest):
    ang = pos_ref[...] * inv_ref[...]
    lane = lax.broadcasted_iota(jnp.int32, ang.shape, 1)
    c = jnp.cos(ang)
    s = jnp.sin(ang)
    rest = 1.0 if keep_rest else 0.0
    c_ref[...] = jnp.where(lane < 2 * half, c, rest)
    s1_ref[...] = jnp.where((lane >= half) & (lane < 2 * half), s, 0.0)
    s2_ref[...] = jnp.where(lane < half, -s, 0.0)


def _rope_tables(pos_f32, rot_dim, keep_rest):
    t = pos_f32.shape[0]
    half = rot_dim // 2
    inv = ROPE_THETA ** (-jnp.arange(0, rot_dim, 2, dtype=F32) / rot_dim)
    inv_row = jnp.concatenate([inv, inv, jnp.zeros((LANES - rot_dim,), F32)])[None, :]
    tm = 1024
    out = jax.ShapeDtypeStruct((t, LANES), F32)
    spec = pl.BlockSpec((tm, LANES), lambda i: (i, 0))
    return pl.pallas_call(
        functools.partial(_rope_table_kernel, half=half, keep_rest=keep_rest),
        grid=(t // tm,),
        in_specs=[pl.BlockSpec((tm, 1), lambda i: (i, 0)),
                  pl.BlockSpec((1, LANES), lambda i: (0, 0))],
        out_specs=[spec, spec, spec],
        out_shape=[out, out, out],
        name="rope_tables",
    )(pos_f32, inv_row)


def _rope(y, c, s1, s2, half):
    return y * c + pltpu.roll(y, half, 1) * s1 + pltpu.roll(y, LANES - half, 1) * s2


def _rms(x, gain, n):
    ms = jnp.sum(x * x, axis=-1, keepdims=True) * (1.0 / n)
    return x * lax.rsqrt(ms + EPS) * gain


def _rmsnorm_kernel(x_ref, g_ref, o_ref):
    o_ref[...] = _rms(x_ref[...], g_ref[...], D_MODEL).astype(o_ref.dtype)


def _rmsnorm(x, g):
    t, d = x.shape
    tm = 256
    return pl.pallas_call(
        _rmsnorm_kernel,
        grid=(t // tm,),
        in_specs=[pl.BlockSpec((tm, d), lambda i: (i, 0)),
                  pl.BlockSpec((1, d), lambda i: (0, 0))],
        out_specs=pl.BlockSpec((tm, d), lambda i: (i, 0)),
        out_shape=jax.ShapeDtypeStruct((t, d), BF16),
        compiler_params=pltpu.CompilerParams(
            dimension_semantics=("arbitrary",),
            vmem_limit_bytes=_vmem_limit([((tm, d), F32), ((tm, d), BF16)])),
        name="rmsnorm",
    )(x, g[None, :])


def _mm_body(*refs, nk, n_extra, n_out, epilogue):
    a_ref, w_ref = refs[0], refs[1]
    extra = refs[2:2 + n_extra]
    outs = refs[2 + n_extra:2 + n_extra + n_out]
    if nk == 1:
        acc = jnp.dot(a_ref[...], w_ref[...], preferred_element_type=F32)
        epilogue(acc, extra, outs)
        return
    acc_ref = refs[-1]
    k = pl.program_id(2)

    @pl.when(k == 0)
    def _():
        acc_ref[...] = jnp.zeros_like(acc_ref)

    acc_ref[...] += jnp.dot(a_ref[...], w_ref[...], preferred_element_type=F32)

    @pl.when(k == nk - 1)
    def _():
        epilogue(acc_ref[...], extra, outs)


def _matmul(a, w, *, tm, tn, tk, epilogue, extras=(), outs, name):
    m, kdim = a.shape
    n = w.shape[1]
    nk = kdim // tk
    assert m % tm == 0 and n % tn == 0 and kdim % tk == 0
    in_specs = [pl.BlockSpec((tm, tk), lambda i, j, k: (i, k)),
                pl.BlockSpec((tk, tn), lambda i, j, k: (k, j))]
    blocks = [((tm, tk), a.dtype), ((tk, tn), w.dtype)]
    for arr, bshape, imap in extras:
        in_specs.append(pl.BlockSpec(bshape, imap))
        blocks.append((bshape, arr.dtype))
    out_specs, out_shape = [], []
    for dtype, cols, bcols in outs:
        out_specs.append(pl.BlockSpec((tm, bcols), lambda i, j, k: (i, j)))
        out_shape.append(jax.ShapeDtypeStruct((m, cols), dtype))
        blocks.append(((tm, bcols), dtype))
    scratch = [pltpu.VMEM((tm, tn), F32)] if nk > 1 else []
    res = pl.pallas_call(
        functools.partial(_mm_body, nk=nk, n_extra=len(extras), n_out=len(outs), epilogue=epilogue),
        grid=(m // tm, n // tn, nk),
        in_specs=in_specs,
        out_specs=out_specs,
        out_shape=out_shape,
        scratch_shapes=scratch,
        compiler_params=pltpu.CompilerParams(
            dimension_semantics=("arbitrary", "arbitrary", "arbitrary"),
            vmem_limit_bytes=_vmem_limit(blocks, [((tm, tn), F32)] if nk > 1 else [])),
        name=name,
    )(a, w, *[e[0] for e in extras])
    return res


def _ep_store(acc, extra, outs):
    outs[0][...] = acc.astype(outs[0].dtype)


def _ep_resid(acc, extra, outs):
    outs[0][...] = extra[0][...] + acc


def _ep_relu2(acc, extra, outs):
    r = jnp.maximum(acc, 0.0)
    outs[0][...] = (r * r).astype(outs[0].dtype)


def _ep_gate(acc, extra, outs):
    res_ref, p_ref, wple_ref = extra
    ple = jnp.dot(p_ref[...], wple_ref[...], preferred_element_type=F32)
    gate = 1.0 / (1.0 + jnp.exp(-acc))
    outs[0][...] = res_ref[...] + gate * ple


def _ep_diff_qk(acc, extra, outs):
    gain_ref, c_ref, s1_ref, s2_ref = extra
    c, s1, s2 = c_ref[...], s1_ref[...], s2_ref[...]
    for g in range(acc.shape[1] // LANES):
        sl = slice(g * LANES, (g + 1) * LANES)
        y = _rms(acc[:, sl], gain_ref[:, sl], DIFF_HEAD_DIM)
        outs[0][:, sl] = _rope(y, c, s1, s2, DIFF_ROT // 2).astype(outs[0].dtype)


def _ep_mla_in(acc, extra, outs):
    gcq_ref, gckv_ref, gkr_ref, c_ref, s1_ref, s2_ref = extra
    cq_ref, ckv_ref, kr_ref = outs
    cq_ref[...] = _rms(acc[:, :MLA_Q_RANK], gcq_ref[...], MLA_Q_RANK).astype(cq_ref.dtype)
    lo = MLA_Q_RANK
    ckv_ref[...] = _rms(acc[:, lo:lo + MLA_KV_RANK], gckv_ref[...], MLA_KV_RANK).astype(ckv_ref.dtype)
    lo += MLA_KV_RANK
    y = _rms(acc[:, lo:lo + LANES], gkr_ref[...], MLA_ROPE)
    kr_ref[...] = _rope(y, c_ref[...], s1_ref[...], s2_ref[...], MLA_ROPE // 2).astype(kr_ref.dtype)


def _ep_mla_q(acc, extra, outs):
    gn_ref, gr_ref, c_ref, s1_ref, s2_ref = extra
    c, s1, s2 = c_ref[...], s1_ref[...], s2_ref[...]
    for h in range(acc.shape[1] // MLA_QK_PAD):
        lo = h * MLA_QK_PAD
        yn = _rms(acc[:, lo:lo + LANES], gn_ref[...], MLA_NOPE)
        outs[0][:, lo:lo + LANES] = yn.astype(outs[0].dtype)
        yr = _rms(acc[:, lo + LANES:lo + 2 * LANES], gr_ref[...], MLA_ROPE)
        outs[0][:, lo + LANES:lo + 2 * LANES] = _rope(yr, c, s1, s2, MLA_ROPE // 2).astype(outs[0].dtype)


def _ep_mla_kv(acc, extra, outs):
    gk_ref, kr_ref, a_ref, wuv_ref = extra
    k_ref, v_ref = outs
    kr = kr_ref[...]
    for h in range(acc.shape[1] // LANES):
        yk = _rms(acc[:, h * LANES:(h + 1) * LANES], gk_ref[...], MLA_NOPE)
        lo = h * MLA_QK_PAD
        k_ref[:, lo:lo + LANES] = yk.astype(k_ref.dtype)
        k_ref[:, lo + LANES:lo + 2 * LANES] = kr
    v_ref[...] = jnp.dot(a_ref[...], wuv_ref[...], preferred_element_type=F32).astype(v_ref.dtype)


def _softmax_pv(q, k_ref, k_cols, v_ref, s_scr, acc_scr, scale, tk):
    nchunk = s_scr.shape[0]
    tq = q.shape[0]

    def scores(j, mpart):
        off = pl.multiple_of(j * tk, tk)
        kc = k_ref[pl.ds(off, tk), k_cols]
        s = lax.dot_general(q, kc, (((1,), (1,)), ((), ())), preferred_element_type=F32)
        s_scr[j] = s
        for g in range(tk // LANES):
            mpart = jnp.maximum(mpart, s[:, g * LANES:(g + 1) * LANES])
        return mpart

    mpart = lax.fori_loop(0, nchunk, scores, jnp.full((tq, LANES), -jnp.inf, F32))
    m = jnp.max(mpart, axis=-1, keepdims=True)
    acc_scr[...] = jnp.zeros_like(acc_scr)

    def weighted(j, lpart):
        off = pl.multiple_of(j * tk, tk)
        e = jnp.exp((s_scr[j] - m) * scale)
        for g in range(tk // LANES):
            lpart = lpart + e[:, g * LANES:(g + 1) * LANES]
        acc_scr[...] += jnp.dot(e.astype(BF16), v_ref[pl.ds(off, tk), :], preferred_element_type=F32)
        return lpart

    lpart = lax.fori_loop(0, nchunk, weighted, jnp.zeros((tq, LANES), F32))
    return acc_scr[...], jnp.sum(lpart, axis=-1, keepdims=True)


def _diff_attn_kernel(lam_ref, gsub_ref, q_ref, k_ref, v_ref, o_ref, s_scr, acc_scr, *, tk, lambda_init):
    lam = lam_ref[...]
    lam_full = (jnp.exp(jnp.sum(lam[0:1] * lam[1:2], axis=-1, keepdims=True))
                - jnp.exp(jnp.sum(lam[2:3] * lam[3:4], axis=-1, keepdims=True)) + lambda_init)
    scale = DIFF_HEAD_DIM ** -0.5
    parts = []
    for c in range(2):
        cols = slice(c * DIFF_HEAD_DIM, (c + 1) * DIFF_HEAD_DIM)
        acc, l = _softmax_pv(q_ref[:, cols], k_ref, cols, v_ref, s_scr, acc_scr, scale, tk)
        parts.append(acc / l)
    o = parts[0] - lam_full * parts[1]
    o = _rms(o, gsub_ref[...], 2 * DIFF_HEAD_DIM) * (1.0 - lambda_init)
    o_ref[...] = o.astype(o_ref.dtype)


def _diff_attention(qk, v, lam, g_sub, lambda_init, batch, seq):
    tq, tk = 512, 512
    nq = seq // tq
    hw = 2 * DIFF_HEAD_DIM
    blocks = [((tq, hw), BF16), ((seq, hw), BF16), ((seq, hw), BF16), ((tq, hw), BF16)]
    scratch = [((seq // tk, tq, tk), F32), ((tq, hw), F32)]
    return pl.pallas_call(
        functools.partial(_diff_attn_kernel, tk=tk, lambda_init=lambda_init),
        grid=(batch, DIFF_HEADS, nq),
        in_specs=[pl.BlockSpec((4, DIFF_HEAD_DIM), lambda b, h, i: (0, 0)),
                  pl.BlockSpec((1, hw), lambda b, h, i: (0, 0)),
                  pl.BlockSpec((tq, hw), lambda b, h, i: (b * nq + i, h)),
                  pl.BlockSpec((seq, hw), lambda b, h, i: (b, DIFF_HEADS + h)),
                  pl.BlockSpec((seq, hw), lambda b, h, i: (b, h))],
        out_specs=pl.BlockSpec((tq, hw), lambda b, h, i: (b * nq + i, h)),
        out_shape=jax.ShapeDtypeStruct((batch * seq, DIFF_V_WIDTH), BF16),
        scratch_shapes=[pltpu.VMEM(s, d) for s, d in scratch],
        compiler_params=pltpu.CompilerParams(
            dimension_semantics=("arbitrary", "arbitrary", "arbitrary"),
            vmem_limit_bytes=_vmem_limit(blocks, scratch)),
        name="diff_attention",
    )(lam, g_sub[None, :], qk, qk, v)


def _mla_attn_kernel(q_ref, k_ref, v_ref, o_ref, s_scr, acc_scr, *, tk):
    scale = MLA_QK ** -0.5
    acc, l = _softmax_pv(q_ref[...], k_ref, slice(None), v_ref, s_scr, acc_scr, scale, tk)
    o_ref[...] = (acc / l).astype(o_ref.dtype)


def _mla_attention(q, k, v, batch, seq):
    tq, tk = 512, 512
    nq = seq // tq
    blocks = [((tq, MLA_QK_PAD), BF16), ((seq, MLA_QK_PAD), BF16), ((seq, MLA_V), BF16), ((tq, MLA_V), BF16)]
    scratch = [((seq // tk, tq, tk), F32), ((tq, MLA_V), F32)]
    return pl.pallas_call(
        functools.partial(_mla_attn_kernel, tk=tk),
        grid=(batch, MLA_HEADS, nq),
        in_specs=[pl.BlockSpec((tq, MLA_QK_PAD), lambda b, h, i: (b * nq + i, h)),
                  pl.BlockSpec((seq, MLA_QK_PAD), lambda b, h, i: (b, h)),
                  pl.BlockSpec((seq, MLA_V), lambda b, h, i: (b, h))],
        out_specs=pl.BlockSpec((tq, MLA_V), lambda b, h, i: (b * nq + i, h)),
        out_shape=jax.ShapeDtypeStruct((batch * seq, MLA_HEADS * MLA_V), BF16),
        scratch_shapes=[pltpu.VMEM(s, d) for s, d in scratch],
        compiler_params=pltpu.CompilerParams(
            dimension_semantics=("arbitrary", "arbitrary", "arbitrary"),
            vmem_limit_bytes=_vmem_limit(blocks, scratch)),
        name="mla_attention",
    )(q, k, v)


TM = 1024
TN = 512


def _row_spec(cols):
    return (TM, cols), (lambda i, j, k: (i, 0))


def _diff_mixer(x, h, w_in, w_out, g_q, g_k, lam, g_sub, tabs, lambda_init, batch, seq):
    w_in = w_in.astype(BF16)
    gain = jnp.concatenate([jnp.tile(g_q, 2 * DIFF_HEADS), jnp.tile(g_k, 2 * DIFF_HEADS)])[None, :]
    tab_extras = [(t, *_row_spec(LANES)) for t in tabs]
    (qk,) = _matmul(h, w_in[:, :2 * DIFF_QK_WIDTH], tm=TM, tn=TN, tk=D_MODEL, epilogue=_ep_diff_qk,
                    extras=[(gain, (1, TN), lambda i, j, k: (0, j))] + tab_extras,
                    outs=[(BF16, 2 * DIFF_QK_WIDTH, TN)], name="diff_qk_proj")
    (v,) = _matmul(h, w_in[:, 2 * DIFF_QK_WIDTH:], tm=TM, tn=TN, tk=D_MODEL, epilogue=_ep_store,
                   outs=[(BF16, DIFF_V_WIDTH, TN)], name="diff_v_proj")
    o = _diff_attention(qk, v, lam, g_sub, lambda_init, batch, seq)
    (x,) = _matmul(o, w_out.astype(BF16), tm=TM, tn=TN, tk=DIFF_V_WIDTH, epilogue=_ep_resid,
                   extras=[(x, (TM, TN), lambda i, j, k: (i, j))],
                   outs=[(F32, D_MODEL, TN)], name="diff_out_proj")
    return x


def _mla_mixer(x, h, w_in, g_cq, g_ckv, w_uq, w_ukv, g_q, g_k, w_out, tabs, batch, seq):
    tab_extras = [(t, *_row_spec(LANES)) for t in tabs]
    zpad = jnp.zeros((MLA_ROPE,), F32)
    n_in = MLA_Q_RANK + MLA_KV_RANK + LANES
    w_in_p = jnp.pad(w_in, ((0, 0), (0, n_in - w_in.shape[1]))).astype(BF16)
    const = lambda i, j, k: (0, 0)
    tm_in = 512
    cq, ckv, kr = _matmul(
        h, w_in_p, tm=tm_in, tn=n_in, tk=D_MODEL, epilogue=_ep_mla_in,
        extras=[(g_cq[None, :], (1, MLA_Q_RANK), const), (g_ckv[None, :], (1, MLA_KV_RANK), const),
                (jnp.concatenate([g_k[MLA_NOPE:], zpad])[None, :], (1, LANES), const)]
               + [(t, (tm_in, LANES), lambda i, j, k: (i, 0)) for t in tabs],
        outs=[(BF16, MLA_Q_RANK, MLA_Q_RANK), (BF16, MLA_KV_RANK, MLA_KV_RANK), (BF16, LANES, LANES)],
        name="mla_in_proj")
    w_uq_h = w_uq.reshape(MLA_Q_RANK, MLA_HEADS, MLA_QK)
    w_uq_p = jnp.pad(w_uq_h, ((0, 0), (0, 0), (0, MLA_QK_PAD - MLA_QK))).reshape(
        MLA_Q_RANK, MLA_HEADS * MLA_QK_PAD).astype(BF16)
    (q,) = _matmul(
        cq, w_uq_p, tm=TM, tn=TN, tk=MLA_Q_RANK, epilogue=_ep_mla_q,
        extras=[(g_q[None, :MLA_NOPE], (1, LANES), const),
                (jnp.concatenate([g_q[MLA_NOPE:], zpad])[None, :], (1, LANES), const)] + tab_extras,
        outs=[(BF16, MLA_HEADS * MLA_QK_PAD, TN)], name="mla_q_proj")
    w_ukv_h = w_ukv.reshape(MLA_KV_RANK, MLA_HEADS, MLA_NOPE + MLA_V).astype(BF16)
    w_uk = w_ukv_h[:, :, :MLA_NOPE].reshape(MLA_KV_RANK, MLA_HEADS * MLA_NOPE)
    w_uv = w_ukv_h[:, :, MLA_NOPE:].reshape(MLA_KV_RANK, MLA_HEADS * MLA_V)
    k, v = _matmul(
        ckv, w_uk, tm=TM, tn=TN, tk=MLA_KV_RANK, epilogue=_ep_mla_kv,
        extras=[(g_k[None, :MLA_NOPE], (1, LANES), const),
                (kr, (TM, LANES), lambda i, j, k: (i, 0)),
                (ckv, (TM, MLA_KV_RANK), lambda i, j, k: (i, 0)),
                (w_uv, (MLA_KV_RANK, TN), lambda i, j, k: (0, j))],
        outs=[(BF16, MLA_HEADS * MLA_QK_PAD, 2 * TN), (BF16, MLA_HEADS * MLA_V, TN)],
        name="mla_kv_proj")
    o = _mla_attention(q, k, v, batch, seq)
    (x,) = _matmul(o, w_out.astype(BF16), tm=TM, tn=TN, tk=MLA_HEADS * MLA_V, epilogue=_ep_resid,
                   extras=[(x, (TM, TN), lambda i, j, k: (i, j))],
                   outs=[(F32, D_MODEL, TN)], name="mla_out_proj")
    return x


def kernel(x, p, positions, g_mix, g_mlp, g_ple, w1, w2, w_gate, w_ple, diff_w_in, diff_w_out, diff_g_q, diff_g_k, diff_lambda, diff_g_sub, mla_w_in, mla_g_cq, mla_g_ckv, mla_w_uq, mla_w_ukv, mla_g_q, mla_g_k, mla_w_out):
    batch, seq, d = x.shape
    t = batch * seq
    x = x.reshape(t, d)
    pos = positions.reshape(t, 1).astype(F32)
    tabs_d = _rope_tables(pos, DIFF_ROT, keep_rest=True)
    tabs_m = _rope_tables(pos, MLA_ROPE, keep_rest=False)
    p_bf = p.reshape(DEPTH, t, PLE_DIM).astype(BF16)
    for i in range(DEPTH):
        j = i // N_MIXERS
        h = _rmsnorm(x, g_mix[i])
        if i % N_MIXERS == 0:
            lambda_init = 0.8 - 0.6 * math.exp(-0.3 * i)
            x = _diff_mixer(x, h, diff_w_in[j], diff_w_out[j], diff_g_q[j], diff_g_k[j], diff_lambda[j],
                            diff_g_sub[j], tabs_d, lambda_init, batch, seq)
        else:
            x = _mla_mixer(x, h, mla_w_in[j], mla_g_cq[j], mla_g_ckv[j], mla_w_uq[j], mla_w_ukv[j],
                           mla_g_q[j], mla_g_k[j], mla_w_out[j], tabs_m, batch, seq)
        h = _rmsnorm(x, g_mlp[i])
        (a,) = _matmul(h, w1[i].astype(BF16), tm=TM, tn=TN, tk=D_MODEL, epilogue=_ep_relu2,
                       outs=[(BF16, D_FF, TN)], name="mlp_up")
        (x,) = _matmul(a, w2[i].astype(BF16), tm=TM, tn=TN, tk=4096, epilogue=_ep_resid,
                       extras=[(x, (TM, TN), lambda i_, j_, k_: (i_, j_))],
                       outs=[(F32, D_MODEL, TN)], name="mlp_down")
        h = _rmsnorm(x, g_ple[i])
        (x,) = _matmul(h, w_gate[i].astype(BF16), tm=TM, tn=TN, tk=D_MODEL, epilogue=_ep_gate,
                       extras=[(x, (TM, TN), lambda i_, j_, k_: (i_, j_)),
                               (p_bf[i], (TM, PLE_DIM), lambda i_, j_, k_: (i_, 0)),
                               (w_ple[i].astype(BF16), (PLE_DIM, TN), lambda i_, j_, k_: (0, j_))],
                       outs=[(F32, D_MODEL, TN)], name="ple_gate")
    return x.reshape(batch, seq, d)
```

```python
import functools
import math

import jax
import jax.numpy as jnp
from jax import lax
from jax.experimental import pallas as pl
from jax.experimental.pallas import tpu as pltpu

D_MODEL = 4096
DEPTH = 4
N_MIXERS = 2
ROPE_THETA = 500000.0
EPS = 1e-6
D_FF = 4 * D_MODEL
PLE_DIM = 256

DIFF_HEAD_DIM = 128
DIFF_HEADS = D_MODEL // (2 * DIFF_HEAD_DIM)
DIFF_ROT = DIFF_HEAD_DIM // 4
DIFF_QK_WIDTH = DIFF_HEADS * 2 * DIFF_HEAD_DIM
DIFF_V_WIDTH = DIFF_HEADS * 2 * DIFF_HEAD_DIM

MLA_HEADS = 32
MLA_NOPE = 128
MLA_ROPE = 64
MLA_V = 128
MLA_QK = MLA_NOPE + MLA_ROPE
MLA_Q_RANK = 1024
MLA_KV_RANK = 512

LANES = 128
MLA_QK_PAD = 2 * LANES
MLA_V_PAD = 2 * LANES
V7X_VMEM_BYTES = 64 * 1024 * 1024
VMEM_TEMP_ALLOWANCE = 12 * 1024 * 1024
LOG2E = math.log2(math.e)

TM = 1024
TN = 512
SUB_ROWS = 256
ATT_TQ = 512
ATT_TK = 512

F32 = jnp.float32
BF16 = jnp.bfloat16


def _nbytes(shape, dtype):
    return math.prod(shape) * jnp.dtype(dtype).itemsize


def _vmem_limit(blocks, scratch=()):
    need = 2 * sum(_nbytes(s, d) for s, d in blocks) + sum(_nbytes(s, d) for s, d in scratch)
    return min(need + VMEM_TEMP_ALLOWANCE, V7X_VMEM_BYTES - 4 * 1024 * 1024)


def _rope_table_kernel(pos_ref, inv_ref, c_ref, s1_ref, s2_ref, *, half, keep_rest):
    ang = pos_ref[...] * inv_ref[...]
    lane = lax.broadcasted_iota(jnp.int32, ang.shape, 1)
    c = jnp.cos(ang)
    s = jnp.sin(ang)
    rest = 1.0 if keep_rest else 0.0
    c_ref[...] = jnp.where(lane < 2 * half, c, rest)
    s1_ref[...] = jnp.where((lane >= half) & (lane < 2 * half), s, 0.0)
    s2_ref[...] = jnp.where(lane < half, -s, 0.0)


def _rope_tables(pos_f32, rot_dim, keep_rest):
    t = pos_f32.shape[0]
    half = rot_dim // 2
    inv = ROPE_THETA ** (-jnp.arange(0, rot_dim, 2, dtype=F32) / rot_dim)
    inv_row = jnp.concatenate([inv, inv, jnp.zeros((LANES - rot_dim,), F32)])[None, :]
    tm = 1024
    out = jax.ShapeDtypeStruct((t, LANES), F32)
    spec = pl.BlockSpec((tm, LANES), lambda i: (i, 0))
    return pl.pallas_call(
        functools.partial(_rope_table_kernel, half=half, keep_rest=keep_rest),
        grid=(t // tm,),
        in_specs=[pl.BlockSpec((tm, 1), lambda i: (i, 0)),
                  pl.BlockSpec((1, LANES), lambda i: (0, 0))],
        out_specs=[spec, spec, spec],
        out_shape=[out, out, out],
        name="rope_tables",
    )(pos_f32, inv_row)


def _rope(y, c, s1, s2, half):
    return y * c + pltpu.roll(y, half, 1) * s1 + pltpu.roll(y, LANES - half, 1) * s2


def _rms(x, gain, n):
    ms = jnp.sum(x * x, axis=-1, keepdims=True) * (1.0 / n)
    return x * lax.rsqrt(ms + EPS) * gain


def _rmsnorm_kernel(x_ref, g_ref, o_ref):
    o_ref[...] = _rms(x_ref[...], g_ref[...], D_MODEL).astype(o_ref.dtype)


def _rmsnorm(x, g):
    t, d = x.shape
    tm = 256
    return pl.pallas_call(
        _rmsnorm_kernel,
        grid=(t // tm,),
        in_specs=[pl.BlockSpec((tm, d), lambda i: (i, 0)),
                  pl.BlockSpec((1, d), lambda i: (0, 0))],
        out_specs=pl.BlockSpec((tm, d), lambda i: (i, 0)),
        out_shape=jax.ShapeDtypeStruct((t, d), BF16),
        compiler_params=pltpu.CompilerParams(
            dimension_semantics=("arbitrary",),
            vmem_limit_bytes=_vmem_limit([((tm, d), F32), ((tm, d), BF16)])),
        name="rmsnorm",
    )(x, g[None, :])


def _mm_body(*refs, nk, n_extra, n_out, epilogue, sub):
    a_ref, w_ref = refs[0], refs[1]
    extra = refs[2:2 + n_extra]
    outs = refs[2 + n_extra:2 + n_extra + n_out]
    tm = a_ref.shape[0]
    if nk == 1:
        for r in range(tm // sub):
            rows = slice(r * sub, (r + 1) * sub)
            acc = jnp.dot(a_ref[rows, :], w_ref[...], preferred_element_type=F32)
            epilogue(acc, extra, outs, rows)
        return
    acc_ref = refs[-1]
    k = pl.program_id(2)

    @pl.when(k == 0)
    def _():
        acc_ref[...] = jnp.zeros_like(acc_ref)

    acc_ref[...] += jnp.dot(a_ref[...], w_ref[...], preferred_element_type=F32)

    @pl.when(k == nk - 1)
    def _():
        epilogue(acc_ref[...], extra, outs, slice(None))


def _matmul(a, w, *, tm, tn, tk, epilogue, extras=(), outs, name, sub=SUB_ROWS):
    m, kdim = a.shape
    n = w.shape[1]
    nk = kdim // tk
    assert m % tm == 0 and n % tn == 0 and kdim % tk == 0 and tm % min(sub, tm) == 0
    in_specs = [pl.BlockSpec((tm, tk), lambda i, j, k: (i, k)),
                pl.BlockSpec((tk, tn), lambda i, j, k: (k, j))]
    blocks = [((tm, tk), a.dtype), ((tk, tn), w.dtype)]
    for arr, bshape, imap in extras:
        in_specs.append(pl.BlockSpec(bshape, imap))
        blocks.append((bshape, arr.dtype))
    out_specs, out_shape = [], []
    for dtype, cols, bcols in outs:
        out_specs.append(pl.BlockSpec((tm, bcols), lambda i, j, k: (i, j)))
        out_shape.append(jax.ShapeDtypeStruct((m, cols), dtype))
        blocks.append(((tm, bcols), dtype))
    scratch = [((tm, tn), F32)] if nk > 1 else []
    return pl.pallas_call(
        functools.partial(_mm_body, nk=nk, n_extra=len(extras), n_out=len(outs), epilogue=epilogue,
                          sub=min(sub, tm)),
        grid=(m // tm, n // tn, nk),
        in_specs=in_specs,
        out_specs=out_specs,
        out_shape=out_shape,
        scratch_shapes=[pltpu.VMEM(s, d) for s, d in scratch],
        compiler_params=pltpu.CompilerParams(
            dimension_semantics=("arbitrary", "arbitrary", "arbitrary"),
            vmem_limit_bytes=_vmem_limit(blocks, scratch)),
        name=name,
    )(a, w, *[e[0] for e in extras])


def _ep_store(acc, extra, outs, rows):
    outs[0][rows, :] = acc.astype(outs[0].dtype)


def _ep_resid(acc, extra, outs, rows):
    outs[0][rows, :] = extra[0][rows, :] + acc


def _ep_relu2(acc, extra, outs, rows):
    r = jnp.maximum(acc, 0.0)
    outs[0][rows, :] = (r * r).astype(outs[0].dtype)


def _ep_gate(acc, extra, outs, rows):
    res_ref, p_ref, wple_ref = extra
    ple = jnp.dot(p_ref[rows, :], wple_ref[...], preferred_element_type=F32)
    gate = 1.0 / (1.0 + jnp.exp(-acc))
    outs[0][rows, :] = res_ref[rows, :] + gate * ple


def _ep_diff_qk(acc, extra, outs, rows):
    gain_ref, c_ref, s1_ref, s2_ref = extra
    c, s1, s2 = c_ref[rows, :], s1_ref[rows, :], s2_ref[rows, :]
    for g in range(acc.shape[1] // LANES):
        sl = slice(g * LANES, (g + 1) * LANES)
        y = _rms(acc[:, sl], gain_ref[:, sl], DIFF_HEAD_DIM)
        outs[0][rows, sl] = _rope(y, c, s1, s2, DIFF_ROT // 2).astype(outs[0].dtype)


def _ep_mla_in(acc, extra, outs, rows):
    gcq_ref, gckv_ref, gkr_ref, c_ref, s1_ref, s2_ref = extra
    cq_ref, ckv_ref, kr_ref = outs
    cq_ref[rows, :] = _rms(acc[:, :MLA_Q_RANK], gcq_ref[...], MLA_Q_RANK).astype(cq_ref.dtype)
    lo = MLA_Q_RANK
    ckv_ref[rows, :] = _rms(acc[:, lo:lo + MLA_KV_RANK], gckv_ref[...], MLA_KV_RANK).astype(ckv_ref.dtype)
    lo += MLA_KV_RANK
    y = _rms(acc[:, lo:lo + LANES], gkr_ref[...], MLA_ROPE)
    kr_ref[rows, :] = _rope(y, c_ref[rows, :], s1_ref[rows, :], s2_ref[rows, :],
                            MLA_ROPE // 2).astype(kr_ref.dtype)


def _ep_mla_q(acc, extra, outs, rows):
    gn_ref, gr_ref, c_ref, s1_ref, s2_ref = extra
    c, s1, s2 = c_ref[rows, :], s1_ref[rows, :], s2_ref[rows, :]
    for h in range(acc.shape[1] // MLA_QK_PAD):
        lo = h * MLA_QK_PAD
        yn = _rms(acc[:, lo:lo + LANES], gn_ref[...], MLA_NOPE)
        outs[0][rows, lo:lo + LANES] = yn.astype(outs[0].dtype)
        yr = _rms(acc[:, lo + LANES:lo + 2 * LANES], gr_ref[...], MLA_ROPE)
        outs[0][rows, lo + LANES:lo + 2 * LANES] = _rope(yr, c, s1, s2, MLA_ROPE // 2).astype(outs[0].dtype)


def _ep_mla_kv(acc, extra, outs, rows):
    gk_ref, kr_ref, a_ref, wuv_ref = extra
    k_ref, v_ref = outs
    kr = kr_ref[rows, :]
    v = jnp.dot(a_ref[rows, :], wuv_ref[...], preferred_element_type=F32).astype(v_ref.dtype)
    ones = jnp.ones((acc.shape[0], LANES), v_ref.dtype)
    for h in range(acc.shape[1] // LANES):
        yk = _rms(acc[:, h * LANES:(h + 1) * LANES], gk_ref[...], MLA_NOPE)
        lo = h * MLA_QK_PAD
        k_ref[rows, lo:lo + LANES] = yk.astype(k_ref.dtype)
        k_ref[rows, lo + LANES:lo + 2 * LANES] = kr
        lo = h * MLA_V_PAD
        v_ref[rows, lo:lo + LANES] = v[:, h * LANES:(h + 1) * LANES]
        v_ref[rows, lo + LANES:lo + 2 * LANES] = ones


def _score_sweep(q, k_ref, k_cols, s_ref, tk):
    mpart = None
    for j in range(s_ref.shape[0]):
        s = lax.dot_general(q, k_ref[j * tk:(j + 1) * tk, k_cols], (((1,), (1,)), ((), ())),
                            preferred_element_type=F32)
        s_ref[j] = s
        for g in range(tk // LANES):
            sg = s[:, g * LANES:(g + 1) * LANES]
            mpart = sg if mpart is None else jnp.maximum(mpart, sg)
    return jnp.max(mpart, axis=-1, keepdims=True)


def _weight_sweep(s_ref, m, v_ref, tk, want_rowsum):
    acc, lpart = None, None
    for j in range(s_ref.shape[0]):
        e = jnp.exp2(s_ref[j] - m)
        if want_rowsum:
            for g in range(tk // LANES):
                eg = e[:, g * LANES:(g + 1) * LANES]
                lpart = eg if lpart is None else lpart + eg
        d = jnp.dot(e.astype(BF16), v_ref[j * tk:(j + 1) * tk, :], preferred_element_type=F32)
        acc = d if acc is None else acc + d
    l = jnp.sum(lpart, axis=-1, keepdims=True) if want_rowsum else None
    return acc, l


def _diff_attn_kernel(lam_ref, gsub_ref, q_ref, k_ref, v_ref, o_ref, s0_ref, s1_ref, m0_ref, m1_ref,
                      *, tq, tk, lambda_init):
    lam = lam_ref[...]
    lam_full = (jnp.exp(jnp.sum(lam[0:1] * lam[1:2], axis=-1, keepdims=True))
                - jnp.exp(jnp.sum(lam[2:3] * lam[3:4], axis=-1, keepdims=True)) + lambda_init)
    ntile = q_ref.shape[0] // tq
    map0 = slice(0, DIFF_HEAD_DIM)
    map1 = slice(DIFF_HEAD_DIM, 2 * DIFF_HEAD_DIM)
    m0_ref[...] = _score_sweep(q_ref[0:tq, map0], k_ref, map0, s0_ref, tk)

    def tile(i, carry):
        rows = pl.ds(pl.multiple_of(i * tq, tq), tq)
        m1_ref[...] = _score_sweep(q_ref[rows, map1], k_ref, map1, s1_ref, tk)
        acc, l = _weight_sweep(s0_ref, m0_ref[...], v_ref, tk, True)
        part0 = acc / l
        nxt = pl.ds(pl.multiple_of(jnp.minimum(i + 1, ntile - 1) * tq, tq), tq)
        m0_ref[...] = _score_sweep(q_ref[nxt, map0], k_ref, map0, s0_ref, tk)
        acc, l = _weight_sweep(s1_ref, m1_ref[...], v_ref, tk, True)
        o = part0 - lam_full * (acc / l)
        o = _rms(o, gsub_ref[...], 2 * DIFF_HEAD_DIM) * (1.0 - lambda_init)
        o_ref[rows, :] = o.astype(o_ref.dtype)
        return carry

    lax.fori_loop(0, ntile, tile, 0)


def _diff_attention(qk, v, lam, g_sub, lambda_init, batch, seq):
    tq, tk = ATT_TQ, ATT_TK
    hw = 2 * DIFF_HEAD_DIM
    blocks = [((seq, hw), BF16)] * 4
    scratch = [((seq // tk, tq, tk), F32)] * 2 + [((tq, 1), F32)] * 2
    return pl.pallas_call(
        functools.partial(_diff_attn_kernel, tq=tq, tk=tk, lambda_init=lambda_init),
        grid=(batch, DIFF_HEADS),
        in_specs=[pl.BlockSpec((4, DIFF_HEAD_DIM), lambda b, h: (0, 0)),
                  pl.BlockSpec((1, hw), lambda b, h: (0, 0)),
                  pl.BlockSpec((seq, hw), lambda b, h: (b, h)),
                  pl.BlockSpec((seq, hw), lambda b, h: (b, DIFF_HEADS + h)),
                  pl.BlockSpec((seq, hw), lambda b, h: (b, h))],
        out_specs=pl.BlockSpec((seq, hw), lambda b, h: (b, h)),
        out_shape=jax.ShapeDtypeStruct((batch * seq, DIFF_V_WIDTH), BF16),
        scratch_shapes=[pltpu.VMEM(s, d) for s, d in scratch],
        compiler_params=pltpu.CompilerParams(
            dimension_semantics=("arbitrary", "arbitrary"),
            vmem_limit_bytes=_vmem_limit(blocks, scratch)),
        name="diff_attention",
    )(lam, g_sub[None, :], qk, qk, v)


def _mla_attn_kernel(q_ref, k_ref, v_ref, o_ref, s0_ref, s1_ref, m0_ref, m1_ref, *, tq, tk):
    npair = q_ref.shape[0] // (2 * tq)
    m0_ref[...] = _score_sweep(q_ref[0:tq, :], k_ref, slice(None), s0_ref, tk)

    def finish(acc, rows):
        o_ref[rows, :] = (acc[:, :MLA_V] / acc[:, MLA_V:MLA_V + 1]).astype(o_ref.dtype)

    def pair(i, carry):
        even = pl.ds(pl.multiple_of(2 * i * tq, tq), tq)
        odd = pl.ds(pl.multiple_of((2 * i + 1) * tq, tq), tq)
        m1_ref[...] = _score_sweep(q_ref[odd, :], k_ref, slice(None), s1_ref, tk)
        acc, _ = _weight_sweep(s0_ref, m0_ref[...], v_ref, tk, False)
        finish(acc, even)
        nxt = pl.ds(pl.multiple_of(2 * jnp.minimum(i + 1, npair - 1) * tq, tq), tq)
        m0_ref[...] = _score_sweep(q_ref[nxt, :], k_ref, slice(None), s0_ref, tk)
        acc, _ = _weight_sweep(s1_ref, m1_ref[...], v_ref, tk, False)
        finish(acc, odd)
        return carry

    lax.fori_loop(0, npair, pair, 0)


def _mla_attention(q, k, v, batch, seq):
    tq, tk = ATT_TQ, ATT_TK
    blocks = [((seq, MLA_QK_PAD), BF16)] * 2 + [((seq, MLA_V_PAD), BF16), ((seq, MLA_V), BF16)]
    scratch = [((seq // tk, tq, tk), F32)] * 2 + [((tq, 1), F32)] * 2
    return pl.pallas_call(
        functools.partial(_mla_attn_kernel, tq=tq, tk=tk),
        grid=(batch, MLA_HEADS),
        in_specs=[pl.BlockSpec((seq, MLA_QK_PAD), lambda b, h: (b, h)),
                  pl.BlockSpec((seq, MLA_QK_PAD), lambda b, h: (b, h)),
                  pl.BlockSpec((seq, MLA_V_PAD), lambda b, h: (b, h))],
        out_specs=pl.BlockSpec((seq, MLA_V), lambda b, h: (b, h)),
        out_shape=jax.ShapeDtypeStruct((batch * seq, MLA_HEADS * MLA_V), BF16),
        scratch_shapes=[pltpu.VMEM(s, d) for s, d in scratch],
        compiler_params=pltpu.CompilerParams(
            dimension_semantics=("arbitrary", "arbitrary"),
            vmem_limit_bytes=_vmem_limit(blocks, scratch)),
        name="mla_attention",
    )(q, k, v)


def _row_spec(cols):
    return (TM, cols), (lambda i, j, k: (i, 0))


def _diff_mixer(x, h, w_in, w_out, g_q, g_k, lam, g_sub, tabs, lambda_init, batch, seq):
    w_in = w_in.astype(BF16)
    qscale = DIFF_HEAD_DIM ** -0.5 * LOG2E
    gain = jnp.concatenate([jnp.tile(g_q * qscale, 2 * DIFF_HEADS), jnp.tile(g_k, 2 * DIFF_HEADS)])[None, :]
    tab_extras = [(t, *_row_spec(LANES)) for t in tabs]
    (qk,) = _matmul(h, w_in[:, :2 * DIFF_QK_WIDTH], tm=TM, tn=TN, tk=D_MODEL, epilogue=_ep_diff_qk,
                    extras=[(gain, (1, TN), lambda i, j, k: (0, j))] + tab_extras,
                    outs=[(BF16, 2 * DIFF_QK_WIDTH, TN)], name="diff_qk_proj")
    (v,) = _matmul(h, w_in[:, 2 * DIFF_QK_WIDTH:], tm=TM, tn=TN, tk=D_MODEL, epilogue=_ep_store,
                   outs=[(BF16, DIFF_V_WIDTH, TN)], name="diff_v_proj")
    o = _diff_attention(qk, v, lam, g_sub, lambda_init, batch, seq)
    (x,) = _matmul(o, w_out.astype(BF16), tm=TM, tn=TN, tk=DIFF_V_WIDTH, epilogue=_ep_resid,
                   extras=[(x, (TM, TN), lambda i, j, k: (i, j))],
                   outs=[(F32, D_MODEL, TN)], name="diff_out_proj")
    return x


def _mla_mixer(x, h, w_in, g_cq, g_ckv, w_uq, w_ukv, g_q, g_k, w_out, tabs, batch, seq):
    tab_extras = [(t, *_row_spec(LANES)) for t in tabs]
    zpad = jnp.zeros((MLA_ROPE,), F32)
    n_in = MLA_Q_RANK + MLA_KV_RANK + LANES
    w_in_p = jnp.pad(w_in, ((0, 0), (0, n_in - w_in.shape[1]))).astype(BF16)
    const = lambda i, j, k: (0, 0)
    tm_in = 512
    cq, ckv, kr = _matmul(
        h, w_in_p, tm=tm_in, tn=n_in, tk=D_MODEL, epilogue=_ep_mla_in,
        extras=[(g_cq[None, :], (1, MLA_Q_RANK), const), (g_ckv[None, :], (1, MLA_KV_RANK), const),
                (jnp.concatenate([g_k[MLA_NOPE:], zpad])[None, :], (1, LANES), const)]
               + [(t, (tm_in, LANES), lambda i, j, k: (i, 0)) for t in tabs],
        outs=[(BF16, MLA_Q_RANK, MLA_Q_RANK), (BF16, MLA_KV_RANK, MLA_KV_RANK), (BF16, LANES, LANES)],
        name="mla_in_proj")
    w_uq_h = w_uq.reshape(MLA_Q_RANK, MLA_HEADS, MLA_QK)
    w_uq_p = jnp.pad(w_uq_h, ((0, 0), (0, 0), (0, MLA_QK_PAD - MLA_QK))).reshape(
        MLA_Q_RANK, MLA_HEADS * MLA_QK_PAD).astype(BF16)
    qscale = MLA_QK ** -0.5 * LOG2E
    g_qs = g_q * qscale
    (q,) = _matmul(
        cq, w_uq_p, tm=TM, tn=TN, tk=MLA_Q_RANK, epilogue=_ep_mla_q,
        extras=[(g_qs[None, :MLA_NOPE], (1, LANES), const),
                (jnp.concatenate([g_qs[MLA_NOPE:], zpad])[None, :], (1, LANES), const)] + tab_extras,
        outs=[(BF16, MLA_HEADS * MLA_QK_PAD, TN)], name="mla_q_proj")
    w_ukv_h = w_ukv.reshape(MLA_KV_RANK, MLA_HEADS, MLA_NOPE + MLA_V).astype(BF16)
    w_uk = w_ukv_h[:, :, :MLA_NOPE].reshape(MLA_KV_RANK, MLA_HEADS * MLA_NOPE)
    w_uv = w_ukv_h[:, :, MLA_NOPE:].reshape(MLA_KV_RANK, MLA_HEADS * MLA_V)
    k, v = _matmul(
        ckv, w_uk, tm=TM, tn=TN, tk=MLA_KV_RANK, epilogue=_ep_mla_kv,
        extras=[(g_k[None, :MLA_NOPE], (1, LANES), const),
                (kr, (TM, LANES), lambda i, j, k: (i, 0)),
                (ckv, (TM, MLA_KV_RANK), lambda i, j, k: (i, 0)),
                (w_uv, (MLA_KV_RANK, TN), lambda i, j, k: (0, j))],
        outs=[(BF16, MLA_HEADS * MLA_QK_PAD, 2 * TN), (BF16, MLA_HEADS * MLA_V_PAD, 2 * TN)],
        name="mla_kv_proj")
    o = _mla_attention(q, k, v, batch, seq)
    (x,) = _matmul(o, w_out.astype(BF16), tm=TM, tn=TN, tk=MLA_HEADS * MLA_V, epilogue=_ep_resid,
                   extras=[(x, (TM, TN), lambda i, j, k: (i, j))],
                   outs=[(F32, D_MODEL, TN)], name="mla_out_proj")
    return x


def kernel(x, p, positions, g_mix, g_mlp, g_ple, w1, w2, w_gate, w_ple, diff_w_in, diff_w_out, diff_g_q, diff_g_k, diff_lambda, diff_g_sub, mla_w_in, mla_g_cq, mla_g_ckv, mla_w_uq, mla_w_ukv, mla_g_q, mla_g_k, mla_w_out):
    batch, seq, d = x.shape
    t = batch * seq
    x = x.reshape(t, d)
    pos = positions.reshape(t, 1).astype(F32)
    tabs_d = _rope_tables(pos, DIFF_ROT, keep_rest=True)
    tabs_m = _rope_tables(pos, MLA_ROPE, keep_rest=False)
    p_bf = p.reshape(DEPTH, t, PLE_DIM).astype(BF16)
    for i in range(DEPTH):
        j = i // N_MIXERS
        h = _rmsnorm(x, g_mix[i])
        if i % N_MIXERS == 0:
            lambda_init = 0.8 - 0.6 * math.exp(-0.3 * i)
            x = _diff_mixer(x, h, diff_w_in[j], diff_w_out[j], diff_g_q[j], diff_g_k[j], diff_lambda[j],
                            diff_g_sub[j], tabs_d, lambda_init, batch, seq)
        else:
            x = _mla_mixer(x, h, mla_w_in[j], mla_g_cq[j], mla_g_ckv[j], mla_w_uq[j], mla_w_ukv[j],
                           mla_g_q[j], mla_g_k[j], mla_w_out[j], tabs_m, batch, seq)
        h = _rmsnorm(x, g_mlp[i])
        (a,) = _matmul(h, w1[i].astype(BF16), tm=TM, tn=TN, tk=D_MODEL, epilogue=_ep_relu2,
                       outs=[(BF16, D_FF, TN)], name="mlp_up")
        (x,) = _matmul(a, w2[i].astype(BF16), tm=TM, tn=TN, tk=4096, epilogue=_ep_resid,
                       extras=[(x, (TM, TN), lambda i_, j_, k_: (i_, j_))],
                       outs=[(F32, D_MODEL, TN)], name="mlp_down")
        h = _rmsnorm(x, g_ple[i])
        (x,) = _matmul(h, w_gate[i].astype(BF16), tm=TM, tn=TN, tk=D_MODEL, epilogue=_ep_gate,
                       extras=[(x, (TM, TN), lambda i_, j_, k_: (i_, j_)),
                               (p_bf[i], (TM, PLE_DIM), lambda i_, j_, k_: (i_, 0)),
                               (w_ple[i].astype(BF16), (PLE_DIM, TN), lambda i_, j_, k_: (0, j_))],
                       outs=[(F32, D_MODEL, TN)], name="ple_gate")
    return x.reshape(batch, seq, d)
```

```python
import functools
import math

import jax
import jax.numpy as jnp
from jax import lax
from jax.experimental import pallas as pl
from jax.experimental.pallas import tpu as pltpu

D_MODEL = 4096
DEPTH = 4
N_MIXERS = 2
ROPE_THETA = 500000.0
EPS = 1e-6
D_FF = 4 * D_MODEL
PLE_DIM = 256

DIFF_HEAD_DIM = 128
DIFF_HEADS = D_MODEL // (2 * DIFF_HEAD_DIM)
DIFF_ROT = DIFF_HEAD_DIM // 4
DIFF_QK_WIDTH = DIFF_HEADS * 2 * DIFF_HEAD_DIM
DIFF_V_WIDTH = DIFF_HEADS * 2 * DIFF_HEAD_DIM

MLA_HEADS = 32
MLA_NOPE = 128
MLA_ROPE = 64
MLA_V = 128
MLA_QK = MLA_NOPE + MLA_ROPE
MLA_Q_RANK = 1024
MLA_KV_RANK = 512

LANES = 128
MLA_QK_PAD = 2 * LANES
MLA_V_PAD = 2 * LANES
V7X_VMEM_BYTES = 64 * 1024 * 1024
VMEM_TEMP_ALLOWANCE = 12 * 1024 * 1024
LOG2E = math.log2(math.e)

TM = 1024
TN = 512
SUB_ROWS = 256
ATT_TQ = 512
ATT_TK = 512

F32 = jnp.float32
BF16 = jnp.bfloat16


def _nbytes(shape, dtype):
    return math.prod(shape) * jnp.dtype(dtype).itemsize


def _vmem_limit(blocks, scratch=()):
    need = 2 * sum(_nbytes(s, d) for s, d in blocks) + sum(_nbytes(s, d) for s, d in scratch)
    return min(need + VMEM_TEMP_ALLOWANCE, V7X_VMEM_BYTES - 4 * 1024 * 1024)


def _rope_table_kernel(pos_ref, inv_ref, c_ref, s1_ref, s2_ref, *, half, keep_rest):
    ang = pos_ref[...] * inv_ref[...]
    lane = lax.broadcasted_iota(jnp.int32, ang.shape, 1)
    c = jnp.cos(ang)
    s = jnp.sin(ang)
    rest = 1.0 if keep_rest else 0.0
    c_ref[...] = jnp.where(lane < 2 * half, c, rest)
    s1_ref[...] = jnp.where((lane >= half) & (lane < 2 * half), s, 0.0)
    s2_ref[...] = jnp.where(lane < half, -s, 0.0)


def _rope_tables(pos_f32, rot_dim, keep_rest):
    t = pos_f32.shape[0]
    half = rot_dim // 2
    inv = ROPE_THETA ** (-jnp.arange(0, rot_dim, 2, dtype=F32) / rot_dim)
    inv_row = jnp.concatenate([inv, inv, jnp.zeros((LANES - rot_dim,), F32)])[None, :]
    tm = 1024
    out = jax.ShapeDtypeStruct((t, LANES), F32)
    spec = pl.BlockSpec((tm, LANES), lambda i: (i, 0))
    return pl.pallas_call(
        functools.partial(_rope_table_kernel, half=half, keep_rest=keep_rest),
        grid=(t // tm,),
        in_specs=[pl.BlockSpec((tm, 1), lambda i: (i, 0)),
                  pl.BlockSpec((1, LANES), lambda i: (0, 0))],
        out_specs=[spec, spec, spec],
        out_shape=[out, out, out],
        name="rope_tables",
    )(pos_f32, inv_row)


def _rope(y, c, s1, s2, half):
    return y * c + pltpu.roll(y, half, 1) * s1 + pltpu.roll(y, LANES - half, 1) * s2


def _rms(x, gain, n):
    ms = jnp.sum(x * x, axis=-1, keepdims=True) * (1.0 / n)
    return x * lax.rsqrt(ms + EPS) * gain


def _rmsnorm_kernel(x_ref, g_ref, o_ref):
    o_ref[...] = _rms(x_ref[...], g_ref[...], D_MODEL).astype(o_ref.dtype)


def _rmsnorm(x, g):
    t, d = x.shape
    tm = 256
    return pl.pallas_call(
        _rmsnorm_kernel,
        grid=(t // tm,),
        in_specs=[pl.BlockSpec((tm, d), lambda i: (i, 0)),
                  pl.BlockSpec((1, d), lambda i: (0, 0))],
        out_specs=pl.BlockSpec((tm, d), lambda i: (i, 0)),
        out_shape=jax.ShapeDtypeStruct((t, d), BF16),
        compiler_params=pltpu.CompilerParams(
            dimension_semantics=("arbitrary",),
            vmem_limit_bytes=_vmem_limit([((tm, d), F32), ((tm, d), BF16)])),
        name="rmsnorm",
    )(x, g[None, :])


def _mm_body(*refs, nk, n_extra, n_out, epilogue, sub):
    a_ref, w_ref = refs[0], refs[1]
    extra = refs[2:2 + n_extra]
    outs = refs[2 + n_extra:2 + n_extra + n_out]
    tm = a_ref.shape[0]
    if nk == 1:
        w = w_ref[...].astype(BF16)
        for r in range(tm // sub):
            rows = slice(r * sub, (r + 1) * sub)
            acc = jnp.dot(a_ref[rows, :], w, preferred_element_type=F32)
            epilogue(acc, extra, outs, rows)
        return
    acc_ref = refs[-1]
    k = pl.program_id(2)

    def partial_dots():
        w = w_ref[...].astype(BF16)
        for r in range(tm // sub):
            rows = slice(r * sub, (r + 1) * sub)
            yield rows, jnp.dot(a_ref[rows, :], w, preferred_element_type=F32)

    @pl.when(k == 0)
    def _():
        for rows, d in partial_dots():
            acc_ref[rows, :] = d

    @pl.when((k > 0) & (k < nk - 1))
    def _():
        for rows, d in partial_dots():
            acc_ref[rows, :] += d

    @pl.when(k == nk - 1)
    def _():
        for rows, d in partial_dots():
            epilogue(acc_ref[rows, :] + d, extra, outs, rows)


def _matmul(a, w, *, tm, tn, tk, epilogue, extras=(), outs, name, sub=SUB_ROWS, layer=None, cols=None):
    m, kdim = a.shape
    col0, n = cols if cols is not None else (0, w.shape[-1])
    nk = kdim // tk
    assert m % tm == 0 and n % tn == 0 and col0 % tn == 0 and kdim % tk == 0 and tm % min(sub, tm) == 0
    j0 = col0 // tn
    if layer is None:
        w_spec = pl.BlockSpec((tk, tn), lambda i, j, k: (k, j0 + j))
    else:
        w_spec = pl.BlockSpec((None, tk, tn), lambda i, j, k: (layer, k, j0 + j))
    in_specs = [pl.BlockSpec((tm, tk), lambda i, j, k: (i, k)), w_spec]
    blocks = [((tm, tk), a.dtype), ((tk, tn), w.dtype)]
    for arr, bshape, imap in extras:
        in_specs.append(pl.BlockSpec(bshape, imap))
        blocks.append((bshape, arr.dtype))
    out_specs, out_shape = [], []
    for dtype, cols, bcols in outs:
        out_specs.append(pl.BlockSpec((tm, bcols), lambda i, j, k: (i, j)))
        out_shape.append(jax.ShapeDtypeStruct((m, cols), dtype))
        blocks.append(((tm, bcols), dtype))
    scratch = [((tm, tn), F32)] if nk > 1 else []
    return pl.pallas_call(
        functools.partial(_mm_body, nk=nk, n_extra=len(extras), n_out=len(outs), epilogue=epilogue,
                          sub=min(sub, tm)),
        grid=(m // tm, n // tn, nk),
        in_specs=in_specs,
        out_specs=out_specs,
        out_shape=out_shape,
        scratch_shapes=[pltpu.VMEM(s, d) for s, d in scratch],
        compiler_params=pltpu.CompilerParams(
            dimension_semantics=("arbitrary", "arbitrary", "arbitrary"),
            vmem_limit_bytes=_vmem_limit(blocks, scratch)),
        name=name,
    )(a, w, *[e[0] for e in extras])


def _ep_store(acc, extra, outs, rows):
    outs[0][rows, :] = acc.astype(outs[0].dtype)


def _ep_resid(acc, extra, outs, rows):
    outs[0][rows, :] = extra[0][rows, :] + acc


def _ep_relu2(acc, extra, outs, rows):
    r = jnp.maximum(acc, 0.0)
    outs[0][rows, :] = (r * r).astype(outs[0].dtype)


def _ep_gate(acc, extra, outs, rows):
    res_ref, p_ref, wple_ref = extra
    ple = jnp.dot(p_ref[rows, :], wple_ref[...], preferred_element_type=F32)
    gate = 1.0 / (1.0 + jnp.exp(-acc))
    outs[0][rows, :] = res_ref[rows, :] + gate * ple


def _ep_diff_qk(acc, extra, outs, rows):
    gain_ref, c_ref, s1_ref, s2_ref = extra
    c, s1, s2 = c_ref[rows, :], s1_ref[rows, :], s2_ref[rows, :]
    for g in range(acc.shape[1] // LANES):
        sl = slice(g * LANES, (g + 1) * LANES)
        y = _rms(acc[:, sl], gain_ref[:, sl], DIFF_HEAD_DIM)
        outs[0][rows, sl] = _rope(y, c, s1, s2, DIFF_ROT // 2).astype(outs[0].dtype)


def _ep_mla_in(acc, extra, outs, rows):
    gcq_ref, gckv_ref, gkr_ref, c_ref, s1_ref, s2_ref = extra
    cq_ref, ckv_ref, kr_ref = outs
    cq_ref[rows, :] = _rms(acc[:, :MLA_Q_RANK], gcq_ref[...], MLA_Q_RANK).astype(cq_ref.dtype)
    lo = MLA_Q_RANK
    ckv_ref[rows, :] = _rms(acc[:, lo:lo + MLA_KV_RANK], gckv_ref[...], MLA_KV_RANK).astype(ckv_ref.dtype)
    lo += MLA_KV_RANK
    y = _rms(acc[:, lo:lo + LANES], gkr_ref[...], MLA_ROPE)
    kr_ref[rows, :] = _rope(y, c_ref[rows, :], s1_ref[rows, :], s2_ref[rows, :],
                            MLA_ROPE // 2).astype(kr_ref.dtype)


def _ep_mla_q(acc, extra, outs, rows):
    gn_ref, gr_ref, c_ref, s1_ref, s2_ref = extra
    c, s1, s2 = c_ref[rows, :], s1_ref[rows, :], s2_ref[rows, :]
    for h in range(acc.shape[1] // MLA_QK_PAD):
        lo = h * MLA_QK_PAD
        yn = _rms(acc[:, lo:lo + LANES], gn_ref[...], MLA_NOPE)
        outs[0][rows, lo:lo + LANES] = yn.astype(outs[0].dtype)
        yr = _rms(acc[:, lo + LANES:lo + 2 * LANES], gr_ref[...], MLA_ROPE)
        outs[0][rows, lo + LANES:lo + 2 * LANES] = _rope(yr, c, s1, s2, MLA_ROPE // 2).astype(outs[0].dtype)


def _ep_mla_kv(acc, extra, outs, rows):
    gk_ref, kr_ref, a_ref, wuv_ref = extra
    k_ref, v_ref = outs
    kr = kr_ref[rows, :]
    v = jnp.dot(a_ref[rows, :], wuv_ref[...], preferred_element_type=F32).astype(v_ref.dtype)
    ones = jnp.ones((acc.shape[0], LANES), v_ref.dtype)
    for h in range(acc.shape[1] // LANES):
        yk = _rms(acc[:, h * LANES:(h + 1) * LANES], gk_ref[...], MLA_NOPE)
        lo = h * MLA_QK_PAD
        k_ref[rows, lo:lo + LANES] = yk.astype(k_ref.dtype)
        k_ref[rows, lo + LANES:lo + 2 * LANES] = kr
        lo = h * MLA_V_PAD
        v_ref[rows, lo:lo + LANES] = v[:, h * LANES:(h + 1) * LANES]
        v_ref[rows, lo + LANES:lo + 2 * LANES] = ones


def _score_sweep(q, k_ref, k_cols, s_ref, tk):
    mpart = None
    for j in range(s_ref.shape[0]):
        s = lax.dot_general(q, k_ref[j * tk:(j + 1) * tk, k_cols], (((1,), (1,)), ((), ())),
                            preferred_element_type=F32)
        s_ref[j] = s
        for g in range(tk // LANES):
            sg = s[:, g * LANES:(g + 1) * LANES]
            mpart = sg if mpart is None else jnp.maximum(mpart, sg)
    return jnp.max(mpart, axis=-1, keepdims=True)


def _fused_sweep(q_next, k_ref, k_cols, s_next_ref, s_ref, m, v_ref, tk, want_rowsum):
    mpart, acc, lpart = None, None, None
    for j in range(s_ref.shape[0]):
        s = lax.dot_general(q_next, k_ref[j * tk:(j + 1) * tk, k_cols], (((1,), (1,)), ((), ())),
                            preferred_element_type=F32)
        s_next_ref[j] = s
        e = jnp.exp2(s_ref[j] - m)
        for g in range(tk // LANES):
            sg = s[:, g * LANES:(g + 1) * LANES]
            mpart = sg if mpart is None else jnp.maximum(mpart, sg)
            if want_rowsum:
                eg = e[:, g * LANES:(g + 1) * LANES]
                lpart = eg if lpart is None else lpart + eg
        d = jnp.dot(e.astype(BF16), v_ref[j * tk:(j + 1) * tk, :], preferred_element_type=F32)
        acc = d if acc is None else acc + d
    l = jnp.sum(lpart, axis=-1, keepdims=True) if want_rowsum else None
    return jnp.max(mpart, axis=-1, keepdims=True), acc, l


def _diff_attn_kernel(lam_ref, gsub_ref, q_ref, k_ref, v_ref, o_ref, s0_ref, s1_ref, m0_ref,
                      *, tq, tk, lambda_init):
    lam = lam_ref[...]
    lam_full = (jnp.exp(jnp.sum(lam[0:1] * lam[1:2], axis=-1, keepdims=True))
                - jnp.exp(jnp.sum(lam[2:3] * lam[3:4], axis=-1, keepdims=True)) + lambda_init)
    ntile = q_ref.shape[0] // tq
    map0 = slice(0, DIFF_HEAD_DIM)
    map1 = slice(DIFF_HEAD_DIM, 2 * DIFF_HEAD_DIM)
    m0_ref[...] = _score_sweep(q_ref[0:tq, map0], k_ref, map0, s0_ref, tk)

    def tile(i, carry):
        rows = pl.ds(pl.multiple_of(i * tq, tq), tq)
        m1, acc, l = _fused_sweep(q_ref[rows, map1], k_ref, map1, s1_ref, s0_ref, m0_ref[...], v_ref, tk, True)
        part0 = acc / l
        nxt = pl.ds(pl.multiple_of(jnp.minimum(i + 1, ntile - 1) * tq, tq), tq)
        m0, acc, l = _fused_sweep(q_ref[nxt, map0], k_ref, map0, s0_ref, s1_ref, m1, v_ref, tk, True)
        m0_ref[...] = m0
        o = part0 - lam_full * (acc / l)
        o = _rms(o, gsub_ref[...], 2 * DIFF_HEAD_DIM) * (1.0 - lambda_init)
        o_ref[rows, :] = o.astype(o_ref.dtype)
        return carry

    lax.fori_loop(0, ntile, tile, 0)


def _diff_attention(qk, v, lam, g_sub, lambda_init, batch, seq):
    tq, tk = ATT_TQ, ATT_TK
    hw = 2 * DIFF_HEAD_DIM
    blocks = [((seq, hw), BF16)] * 4
    scratch = [((seq // tk, tq, tk), F32)] * 2 + [((tq, 1), F32)]
    return pl.pallas_call(
        functools.partial(_diff_attn_kernel, tq=tq, tk=tk, lambda_init=lambda_init),
        grid=(batch, DIFF_HEADS),
        in_specs=[pl.BlockSpec((4, DIFF_HEAD_DIM), lambda b, h: (0, 0)),
                  pl.BlockSpec((1, hw), lambda b, h: (0, 0)),
                  pl.BlockSpec((seq, hw), lambda b, h: (b, h)),
                  pl.BlockSpec((seq, hw), lambda b, h: (b, DIFF_HEADS + h)),
                  pl.BlockSpec((seq, hw), lambda b, h: (b, h))],
        out_specs=pl.BlockSpec((seq, hw), lambda b, h: (b, h)),
        out_shape=jax.ShapeDtypeStruct((batch * seq, DIFF_V_WIDTH), BF16),
        scratch_shapes=[pltpu.VMEM(s, d) for s, d in scratch],
        compiler_params=pltpu.CompilerParams(
            dimension_semantics=("arbitrary", "arbitrary"),
            vmem_limit_bytes=_vmem_limit(blocks, scratch)),
        name="diff_attention",
    )(lam, g_sub[None, :], qk, qk, v)


def _mla_attn_kernel(q_ref, k_ref, v_ref, o_ref, s0_ref, s1_ref, m0_ref, *, tq, tk):
    npair = q_ref.shape[0] // (2 * tq)
    m0_ref[...] = _score_sweep(q_ref[0:tq, :], k_ref, slice(None), s0_ref, tk)

    def finish(acc, rows):
        o_ref[rows, :] = (acc[:, :MLA_V] / acc[:, MLA_V:MLA_V + 1]).astype(o_ref.dtype)

    def pair(i, carry):
        even = pl.ds(pl.multiple_of(2 * i * tq, tq), tq)
        odd = pl.ds(pl.multiple_of((2 * i + 1) * tq, tq), tq)
        m1, acc, _ = _fused_sweep(q_ref[odd, :], k_ref, slice(None), s1_ref, s0_ref, m0_ref[...], v_ref, tk, False)
        finish(acc, even)
        nxt = pl.ds(pl.multiple_of(2 * jnp.minimum(i + 1, npair - 1) * tq, tq), tq)
        m0, acc, _ = _fused_sweep(q_ref[nxt, :], k_ref, slice(None), s0_ref, s1_ref, m1, v_ref, tk, False)
        m0_ref[...] = m0
        finish(acc, odd)
        return carry

    lax.fori_loop(0, npair, pair, 0)


def _mla_attention(q, k, v, batch, seq):
    tq, tk = ATT_TQ, ATT_TK
    blocks = [((seq, MLA_QK_PAD), BF16)] * 2 + [((seq, MLA_V_PAD), BF16), ((seq, MLA_V), BF16)]
    scratch = [((seq // tk, tq, tk), F32)] * 2 + [((tq, 1), F32)]
    return pl.pallas_call(
        functools.partial(_mla_attn_kernel, tq=tq, tk=tk),
        grid=(batch, MLA_HEADS),
        in_specs=[pl.BlockSpec((seq, MLA_QK_PAD), lambda b, h: (b, h)),
                  pl.BlockSpec((seq, MLA_QK_PAD), lambda b, h: (b, h)),
                  pl.BlockSpec((seq, MLA_V_PAD), lambda b, h: (b, h))],
        out_specs=pl.BlockSpec((seq, MLA_V), lambda b, h: (b, h)),
        out_shape=jax.ShapeDtypeStruct((batch * seq, MLA_HEADS * MLA_V), BF16),
        scratch_shapes=[pltpu.VMEM(s, d) for s, d in scratch],
        compiler_params=pltpu.CompilerParams(
            dimension_semantics=("arbitrary", "arbitrary"),
            vmem_limit_bytes=_vmem_limit(blocks, scratch)),
        name="mla_attention",
    )(q, k, v)


def _row_spec(cols):
    return (TM, cols), (lambda i, j, k: (i, 0))


def _diff_mixer(x, h, w_in, w_out, layer, g_q, g_k, lam, g_sub, tabs, lambda_init, batch, seq):
    qscale = DIFF_HEAD_DIM ** -0.5 * LOG2E
    gain = jnp.concatenate([jnp.tile(g_q * qscale, 2 * DIFF_HEADS), jnp.tile(g_k, 2 * DIFF_HEADS)])[None, :]
    tab_extras = [(t, *_row_spec(LANES)) for t in tabs]
    (qk,) = _matmul(h, w_in, layer=layer, cols=(0, 2 * DIFF_QK_WIDTH), tm=TM, tn=TN, tk=D_MODEL,
                    epilogue=_ep_diff_qk,
                    extras=[(gain, (1, TN), lambda i, j, k: (0, j))] + tab_extras,
                    outs=[(BF16, 2 * DIFF_QK_WIDTH, TN)], name="diff_qk_proj")
    (v,) = _matmul(h, w_in, layer=layer, cols=(2 * DIFF_QK_WIDTH, DIFF_V_WIDTH), tm=TM, tn=TN, tk=D_MODEL,
                   epilogue=_ep_store, outs=[(BF16, DIFF_V_WIDTH, TN)], name="diff_v_proj")
    o = _diff_attention(qk, v, lam, g_sub, lambda_init, batch, seq)
    (x,) = _matmul(o, w_out, layer=layer, tm=TM, tn=TN, tk=DIFF_V_WIDTH, epilogue=_ep_resid,
                   extras=[(x, (TM, TN), lambda i, j, k: (i, j))],
                   outs=[(F32, D_MODEL, TN)], name="diff_out_proj")
    return x


def _mla_mixer(x, h, w_in, g_cq, g_ckv, w_uq, w_ukv, g_q, g_k, w_out, layer, tabs, batch, seq):
    tab_extras = [(t, *_row_spec(LANES)) for t in tabs]
    zpad = jnp.zeros((MLA_ROPE,), F32)
    n_in = MLA_Q_RANK + MLA_KV_RANK + LANES
    w_in_p = jnp.pad(w_in, ((0, 0), (0, n_in - w_in.shape[1]))).astype(BF16)
    const = lambda i, j, k: (0, 0)
    tm_in = 512
    cq, ckv, kr = _matmul(
        h, w_in_p, tm=tm_in, tn=n_in, tk=D_MODEL, epilogue=_ep_mla_in,
        extras=[(g_cq[None, :], (1, MLA_Q_RANK), const), (g_ckv[None, :], (1, MLA_KV_RANK), const),
                (jnp.concatenate([g_k[MLA_NOPE:], zpad])[None, :], (1, LANES), const)]
               + [(t, (tm_in, LANES), lambda i, j, k: (i, 0)) for t in tabs],
        outs=[(BF16, MLA_Q_RANK, MLA_Q_RANK), (BF16, MLA_KV_RANK, MLA_KV_RANK), (BF16, LANES, LANES)],
        name="mla_in_proj")
    w_uq_h = w_uq.reshape(MLA_Q_RANK, MLA_HEADS, MLA_QK)
    w_uq_p = jnp.pad(w_uq_h, ((0, 0), (0, 0), (0, MLA_QK_PAD - MLA_QK))).reshape(
        MLA_Q_RANK, MLA_HEADS * MLA_QK_PAD).astype(BF16)
    qscale = MLA_QK ** -0.5 * LOG2E
    g_qs = g_q * qscale
    (q,) = _matmul(
        cq, w_uq_p, tm=TM, tn=TN, tk=MLA_Q_RANK, epilogue=_ep_mla_q,
        extras=[(g_qs[None, :MLA_NOPE], (1, LANES), const),
                (jnp.concatenate([g_qs[MLA_NOPE:], zpad])[None, :], (1, LANES), const)] + tab_extras,
        outs=[(BF16, MLA_HEADS * MLA_QK_PAD, TN)], name="mla_q_proj")
    w_ukv_h = w_ukv.reshape(MLA_KV_RANK, MLA_HEADS, MLA_NOPE + MLA_V).astype(BF16)
    w_uk = w_ukv_h[:, :, :MLA_NOPE].reshape(MLA_KV_RANK, MLA_HEADS * MLA_NOPE)
    w_uv = w_ukv_h[:, :, MLA_NOPE:].reshape(MLA_KV_RANK, MLA_HEADS * MLA_V)
    k, v = _matmul(
        ckv, w_uk, tm=TM, tn=TN, tk=MLA_KV_RANK, epilogue=_ep_mla_kv,
        extras=[(g_k[None, :MLA_NOPE], (1, LANES), const),
                (kr, (TM, LANES), lambda i, j, k: (i, 0)),
                (ckv, (TM, MLA_KV_RANK), lambda i, j, k: (i, 0)),
                (w_uv, (MLA_KV_RANK, TN), lambda i, j, k: (0, j))],
        outs=[(BF16, MLA_HEADS * MLA_QK_PAD, 2 * TN), (BF16, MLA_HEADS * MLA_V_PAD, 2 * TN)],
        name="mla_kv_proj")
    o = _mla_attention(q, k, v, batch, seq)
    (x,) = _matmul(o, w_out, layer=layer, tm=TM, tn=TN, tk=MLA_HEADS * MLA_V, epilogue=_ep_resid,
                   extras=[(x, (TM, TN), lambda i, j, k: (i, j))],
                   outs=[(F32, D_MODEL, TN)], name="mla_out_proj")
    return x


def kernel(x, p, positions, g_mix, g_mlp, g_ple, w1, w2, w_gate, w_ple, diff_w_in, diff_w_out, diff_g_q, diff_g_k, diff_lambda, diff_g_sub, mla_w_in, mla_g_cq, mla_g_ckv, mla_w_uq, mla_w_ukv, mla_g_q, mla_g_k, mla_w_out):
    batch, seq, d = x.shape
    t = batch * seq
    x = x.reshape(t, d)
    pos = positions.reshape(t, 1).astype(F32)
    tabs_d = _rope_tables(pos, DIFF_ROT, keep_rest=True)
    tabs_m = _rope_tables(pos, MLA_ROPE, keep_rest=False)
    p_bf = p.reshape(DEPTH, t, PLE_DIM).astype(BF16)
    for i in range(DEPTH):
        j = i // N_MIXERS
        h = _rmsnorm(x, g_mix[i])
        if i % N_MIXERS == 0:
            lambda_init = 0.8 - 0.6 * math.exp(-0.3 * i)
            x = _diff_mixer(x, h, diff_w_in, diff_w_out, j, diff_g_q[j], diff_g_k[j], diff_lambda[j],
                            diff_g_sub[j], tabs_d, lambda_init, batch, seq)
        else:
            x = _mla_mixer(x, h, mla_w_in[j], mla_g_cq[j], mla_g_ckv[j], mla_w_uq[j], mla_w_ukv[j],
                           mla_g_q[j], mla_g_k[j], mla_w_out, j, tabs_m, batch, seq)
        h = _rmsnorm(x, g_mlp[i])
        (a,) = _matmul(h, w1, layer=i, tm=TM, tn=TN, tk=D_MODEL, epilogue=_ep_relu2,
                       outs=[(BF16, D_FF, TN)], name="mlp_up")
        (x,) = _matmul(a, w2[i].astype(BF16), tm=TM, tn=TN, tk=4096, epilogue=_ep_resid,
                       extras=[(x, (TM, TN), lambda i_, j_, k_: (i_, j_))],
                       outs=[(F32, D_MODEL, TN)], name="mlp_down")
        h = _rmsnorm(x, g_ple[i])
        (x,) = _matmul(h, w_gate, layer=i, tm=TM, tn=TN, tk=D_MODEL, epilogue=_ep_gate,
                       extras=[(x, (TM, TN), lambda i_, j_, k_: (i_, j_)),
                               (p_bf[i], (TM, PLE_DIM), lambda i_, j_, k_: (i_, 0)),
                               (w_ple[i].astype(BF16), (PLE_DIM, TN), lambda i_, j_, k_: (0, j_))],
                       outs=[(F32, D_MODEL, TN)], name="ple_gate")
    return x.reshape(batch, seq, d)
```

```python
import functools
import math

import jax
import jax.numpy as jnp
from jax import lax
from jax.experimental import pallas as pl
from jax.experimental.pallas import tpu as pltpu

D_MODEL = 4096
DEPTH = 4
N_MIXERS = 2
ROPE_THETA = 500000.0
EPS = 1e-6
D_FF = 4 * D_MODEL
PLE_DIM = 256

DIFF_HEAD_DIM = 128
DIFF_HEADS = D_MODEL // (2 * DIFF_HEAD_DIM)
DIFF_ROT = DIFF_HEAD_DIM // 4
DIFF_QK_WIDTH = DIFF_HEADS * 2 * DIFF_HEAD_DIM
DIFF_V_WIDTH = DIFF_HEADS * 2 * DIFF_HEAD_DIM

MLA_HEADS = 32
MLA_NOPE = 128
MLA_ROPE = 64
MLA_V = 128
MLA_QK = MLA_NOPE + MLA_ROPE
MLA_Q_RANK = 1024
MLA_KV_RANK = 512

LANES = 128
MLA_QK_PAD = 2 * LANES
MLA_V_PAD = 2 * LANES
V7X_VMEM_BYTES = 64 * 1024 * 1024
VMEM_TEMP_ALLOWANCE = 12 * 1024 * 1024
LOG2E = math.log2(math.e)

TM = 1024
TN = 512
SUB_ROWS = 256
ATT_TQ = 512
ATT_TK = 512

F32 = jnp.float32
BF16 = jnp.bfloat16


def _nbytes(shape, dtype):
    return math.prod(shape) * jnp.dtype(dtype).itemsize


def _vmem_limit(blocks, scratch=()):
    need = 2 * sum(_nbytes(s, d) for s, d in blocks) + sum(_nbytes(s, d) for s, d in scratch)
    return min(need + VMEM_TEMP_ALLOWANCE, V7X_VMEM_BYTES - 4 * 1024 * 1024)


def _rope_table_kernel(pos_ref, inv_ref, c_ref, s1_ref, s2_ref, *, half, keep_rest):
    ang = pos_ref[...] * inv_ref[...]
    lane = lax.broadcasted_iota(jnp.int32, ang.shape, 1)
    c = jnp.cos(ang)
    s = jnp.sin(ang)
    rest = 1.0 if keep_rest else 0.0
    c_ref[...] = jnp.where(lane < 2 * half, c, rest)
    s1_ref[...] = jnp.where((lane >= half) & (lane < 2 * half), s, 0.0)
    s2_ref[...] = jnp.where(lane < half, -s, 0.0)


def _rope_tables(pos_f32, rot_dim, keep_rest):
    t = pos_f32.shape[0]
    half = rot_dim // 2
    inv = ROPE_THETA ** (-jnp.arange(0, rot_dim, 2, dtype=F32) / rot_dim)
    inv_row = jnp.concatenate([inv, inv, jnp.zeros((LANES - rot_dim,), F32)])[None, :]
    tm = 1024
    out = jax.ShapeDtypeStruct((t, LANES), F32)
    spec = pl.BlockSpec((tm, LANES), lambda i: (i, 0))
    return pl.pallas_call(
        functools.partial(_rope_table_kernel, half=half, keep_rest=keep_rest),
        grid=(t // tm,),
        in_specs=[pl.BlockSpec((tm, 1), lambda i: (i, 0)),
                  pl.BlockSpec((1, LANES), lambda i: (0, 0))],
        out_specs=[spec, spec, spec],
        out_shape=[out, out, out],
        name="rope_tables",
    )(pos_f32, inv_row)


def _rope(y, c, s1, s2, half):
    return y * c + pltpu.roll(y, half, 1) * s1 + pltpu.roll(y, LANES - half, 1) * s2


def _rms(x, gain, n):
    ms = jnp.sum(x * x, axis=-1, keepdims=True) * (1.0 / n)
    return x * lax.rsqrt(ms + EPS) * gain


def _stream_stats_kernel(x_ref, xb_ref, ssq_ref):
    x = x_ref[...]
    xb_ref[...] = x.astype(BF16)
    sq = x * x
    part = sq[:, :LANES]
    for c in range(1, sq.shape[1] // LANES):
        part = part + sq[:, c * LANES:(c + 1) * LANES]
    ssq_ref[...] = part


def _stream_stats(x):
    t, d = x.shape
    tm = 256
    return pl.pallas_call(
        _stream_stats_kernel,
        grid=(t // tm,),
        in_specs=[pl.BlockSpec((tm, d), lambda i: (i, 0))],
        out_specs=[pl.BlockSpec((tm, d), lambda i: (i, 0)), pl.BlockSpec((tm, LANES), lambda i: (i, 0))],
        out_shape=[jax.ShapeDtypeStruct((t, d), BF16), jax.ShapeDtypeStruct((t, LANES), F32)],
        compiler_params=pltpu.CompilerParams(
            dimension_semantics=("arbitrary",),
            vmem_limit_bytes=_vmem_limit([((tm, d), F32), ((tm, d), BF16), ((tm, LANES), F32)])),
        name="stream_stats",
    )(x)


def _mm_body(*refs, nk, n_extra, n_out, epilogue, sub, has_norm, has_gain, stats):
    a_ref, w_ref = refs[0], refs[1]
    pos = 2
    ssq_ref = refs[pos] if has_norm else None
    pos += has_norm
    gain_ref = refs[pos] if has_gain else None
    pos += has_gain
    extra = refs[pos:pos + n_extra]
    outs = refs[pos + n_extra:pos + n_extra + n_out]
    tm = a_ref.shape[0]

    if stats:
        @pl.when((pl.program_id(1) == 0) & (pl.program_id(2) == 0))
        def _():
            outs[-1][...] = jnp.zeros_like(outs[-1])

    def weights():
        w = w_ref[...]
        if has_gain:
            g = gain_ref[...]
            w = jnp.concatenate([w[:, c * LANES:(c + 1) * LANES] * g for c in range(w.shape[1] // LANES)], axis=1)
        return w.astype(BF16)

    def scaled(acc, rows):
        if not has_norm:
            return acc
        ms = jnp.sum(ssq_ref[rows, :], axis=-1, keepdims=True) * (1.0 / D_MODEL)
        return acc * lax.rsqrt(ms + EPS)

    if nk == 1:
        w = weights()
        for r in range(tm // sub):
            rows = slice(r * sub, (r + 1) * sub)
            acc = jnp.dot(a_ref[rows, :], w, preferred_element_type=F32)
            epilogue(scaled(acc, rows), extra, outs, rows)
        return
    assert not has_norm
    acc_ref = refs[-1]
    k = pl.program_id(2)

    def partial_dots():
        w = weights()
        for r in range(tm // sub):
            rows = slice(r * sub, (r + 1) * sub)
            yield rows, jnp.dot(a_ref[rows, :], w, preferred_element_type=F32)

    @pl.when(k == 0)
    def _():
        for rows, d in partial_dots():
            acc_ref[rows, :] = d

    @pl.when((k > 0) & (k < nk - 1))
    def _():
        for rows, d in partial_dots():
            acc_ref[rows, :] += d

    @pl.when(k == nk - 1)
    def _():
        for rows, d in partial_dots():
            epilogue(acc_ref[rows, :] + d, extra, outs, rows)


def _matmul(a, w, *, tm, tn, tk, epilogue, extras=(), outs, name, sub=SUB_ROWS, layer=None, cols=None,
            norm=None, gain=None, stats=False):
    m, kdim = a.shape
    col0, n = cols if cols is not None else (0, w.shape[-1])
    nk = kdim // tk
    assert m % tm == 0 and n % tn == 0 and col0 % tn == 0 and kdim % tk == 0 and tm % min(sub, tm) == 0
    j0 = col0 // tn
    if layer is None:
        w_spec = pl.BlockSpec((tk, tn), lambda i, j, k: (k, j0 + j))
    else:
        w_spec = pl.BlockSpec((None, tk, tn), lambda i, j, k: (layer, k, j0 + j))
    in_specs = [pl.BlockSpec((tm, tk), lambda i, j, k: (i, k)), w_spec]
    blocks = [((tm, tk), a.dtype), ((tk, tn), w.dtype)]
    operands = [a, w]
    if norm is not None:
        extras = [(norm, (tm, LANES), lambda i, j, k: (i, 0))] + list(extras)
    if gain is not None:
        pos = 1 if norm is not None else 0
        extras = list(extras[:pos]) + [(gain, (tk, LANES), lambda i, j, k: (k, 0))] + list(extras[pos:])
    for arr, bshape, imap in extras:
        in_specs.append(pl.BlockSpec(bshape, imap))
        blocks.append((bshape, arr.dtype))
        operands.append(arr)
    out_specs, out_shape = [], []
    for dtype, ocols, bcols in outs:
        out_specs.append(pl.BlockSpec((tm, bcols), lambda i, j, k: (i, j)))
        out_shape.append(jax.ShapeDtypeStruct((m, ocols), dtype))
        blocks.append(((tm, bcols), dtype))
    if stats:
        out_specs.append(pl.BlockSpec((tm, tn), lambda i, j, k: (i, j)))
        out_shape.append(jax.ShapeDtypeStruct((m, n), BF16))
        out_specs.append(pl.BlockSpec((tm, LANES), lambda i, j, k: (i, 0)))
        out_shape.append(jax.ShapeDtypeStruct((m, LANES), F32))
        blocks += [((tm, tn), BF16), ((tm, LANES), F32)]
    scratch = [((tm, tn), F32)] if nk > 1 else []
    n_body = (norm is not None) + (gain is not None)
    return pl.pallas_call(
        functools.partial(_mm_body, nk=nk, n_extra=len(extras) - n_body, n_out=len(out_specs),
                          epilogue=epilogue, sub=min(sub, tm), has_norm=norm is not None,
                          has_gain=gain is not None, stats=stats),
        grid=(m // tm, n // tn, nk),
        in_specs=in_specs,
        out_specs=out_specs,
        out_shape=out_shape,
        scratch_shapes=[pltpu.VMEM(s, d) for s, d in scratch],
        compiler_params=pltpu.CompilerParams(
            dimension_semantics=("arbitrary", "arbitrary", "arbitrary"),
            vmem_limit_bytes=_vmem_limit(blocks, scratch)),
        name=name,
    )(*operands)


def _emit_stream(x_new, outs, rows):
    outs[0][rows, :] = x_new
    if len(outs) == 3:
        outs[1][rows, :] = x_new.astype(BF16)
        sq = x_new * x_new
        part = sq[:, :LANES]
        for c in range(1, sq.shape[1] // LANES):
            part = part + sq[:, c * LANES:(c + 1) * LANES]
        outs[2][rows, :] += part


def _ep_store(acc, extra, outs, rows):
    outs[0][rows, :] = acc.astype(outs[0].dtype)


def _ep_resid(acc, extra, outs, rows):
    _emit_stream(extra[0][rows, :] + acc, outs, rows)


def _ep_relu2(acc, extra, outs, rows):
    r = jnp.maximum(acc, 0.0)
    outs[0][rows, :] = (r * r).astype(outs[0].dtype)


def _ep_gate(acc, extra, outs, rows):
    res_ref, p_ref, wple_ref = extra
    ple = jnp.dot(p_ref[rows, :], wple_ref[...], preferred_element_type=F32)
    gate = 1.0 / (1.0 + jnp.exp(-acc))
    _emit_stream(res_ref[rows, :] + gate * ple, outs, rows)


def _ep_diff_qk(acc, extra, outs, rows):
    gain_ref, c_ref, s1_ref, s2_ref = extra
    c, s1, s2 = c_ref[rows, :], s1_ref[rows, :], s2_ref[rows, :]
    for g in range(acc.shape[1] // LANES):
        sl = slice(g * LANES, (g + 1) * LANES)
        y = _rms(acc[:, sl], gain_ref[:, sl], DIFF_HEAD_DIM)
        outs[0][rows, sl] = _rope(y, c, s1, s2, DIFF_ROT // 2).astype(outs[0].dtype)


def _ep_mla_in(acc, extra, outs, rows):
    gcq_ref, gckv_ref, gkr_ref, c_ref, s1_ref, s2_ref = extra
    cq_ref, ckv_ref, kr_ref = outs
    cq_ref[rows, :] = _rms(acc[:, :MLA_Q_RANK], gcq_ref[...], MLA_Q_RANK).astype(cq_ref.dtype)
    lo = MLA_Q_RANK
    ckv_ref[rows, :] = _rms(acc[:, lo:lo + MLA_KV_RANK], gckv_ref[...], MLA_KV_RANK).astype(ckv_ref.dtype)
    lo += MLA_KV_RANK
    y = _rms(acc[:, lo:lo + LANES], gkr_ref[...], MLA_ROPE)
    kr_ref[rows, :] = _rope(y, c_ref[rows, :], s1_ref[rows, :], s2_ref[rows, :],
                            MLA_ROPE // 2).astype(kr_ref.dtype)


def _ep_mla_q(acc, extra, outs, rows):
    gn_ref, gr_ref, c_ref, s1_ref, s2_ref = extra
    c, s1, s2 = c_ref[rows, :], s1_ref[rows, :], s2_ref[rows, :]
    for h in range(acc.shape[1] // MLA_QK_PAD):
        lo = h * MLA_QK_PAD
        yn = _rms(acc[:, lo:lo + LANES], gn_ref[...], MLA_NOPE)
        outs[0][rows, lo:lo + LANES] = yn.astype(outs[0].dtype)
        yr = _rms(acc[:, lo + LANES:lo + 2 * LANES], gr_ref[...], MLA_ROPE)
        outs[0][rows, lo + LANES:lo + 2 * LANES] = _rope(yr, c, s1, s2, MLA_ROPE // 2).astype(outs[0].dtype)


def _ep_mla_kv(acc, extra, outs, rows):
    gk_ref, kr_ref, a_ref, wuv_ref = extra
    k_ref, v_ref = outs
    kr = kr_ref[rows, :]
    v = jnp.dot(a_ref[rows, :], wuv_ref[...], preferred_element_type=F32).astype(v_ref.dtype)
    ones = jnp.ones((acc.shape[0], LANES), v_ref.dtype)
    for h in range(acc.shape[1] // LANES):
        yk = _rms(acc[:, h * LANES:(h + 1) * LANES], gk_ref[...], MLA_NOPE)
        lo = h * MLA_QK_PAD
        k_ref[rows, lo:lo + LANES] = yk.astype(k_ref.dtype)
        k_ref[rows, lo + LANES:lo + 2 * LANES] = kr
        lo = h * MLA_V_PAD
        v_ref[rows, lo:lo + LANES] = v[:, h * LANES:(h + 1) * LANES]
        v_ref[rows, lo + LANES:lo + 2 * LANES] = ones


def _score_sweep(q, k_ref, k_cols, s_ref, tk):
    mpart = None
    for j in range(s_ref.shape[0]):
        s = lax.dot_general(q, k_ref[j * tk:(j + 1) * tk, k_cols], (((1,), (1,)), ((), ())),
                            preferred_element_type=F32)
        s_ref[j] = s
        for g in range(tk // LANES):
            sg = s[:, g * LANES:(g + 1) * LANES]
            mpart = sg if mpart is None else jnp.maximum(mpart, sg)
    return jnp.max(mpart, axis=-1, keepdims=True)


def _fused_sweep(q_next, k_ref, k_cols, s_next_ref, s_ref, m, v_ref, tk, want_rowsum):
    mpart, acc, lpart = None, None, None
    for j in range(s_ref.shape[0]):
        s = lax.dot_general(q_next, k_ref[j * tk:(j + 1) * tk, k_cols], (((1,), (1,)), ((), ())),
                            preferred_element_type=F32)
        s_next_ref[j] = s
        e = jnp.exp2(s_ref[j] - m)
        for g in range(tk // LANES):
            sg = s[:, g * LANES:(g + 1) * LANES]
            mpart = sg if mpart is None else jnp.maximum(mpart, sg)
            if want_rowsum:
                eg = e[:, g * LANES:(g + 1) * LANES]
                lpart = eg if lpart is None else lpart + eg
        d = jnp.dot(e.astype(BF16), v_ref[j * tk:(j + 1) * tk, :], preferred_element_type=F32)
        acc = d if acc is None else acc + d
    l = jnp.sum(lpart, axis=-1, keepdims=True) if want_rowsum else None
    return jnp.max(mpart, axis=-1, keepdims=True), acc, l


def _diff_attn_kernel(lam_ref, gsub_ref, q_ref, k_ref, v_ref, o_ref, s0_ref, s1_ref, m0_ref,
                      *, tq, tk, lambda_init):
    lam = lam_ref[...]
    lam_full = (jnp.exp(jnp.sum(lam[0:1] * lam[1:2], axis=-1, keepdims=True))
                - jnp.exp(jnp.sum(lam[2:3] * lam[3:4], axis=-1, keepdims=True)) + lambda_init)
    ntile = q_ref.shape[0] // tq
    map0 = slice(0, DIFF_HEAD_DIM)
    map1 = slice(DIFF_HEAD_DIM, 2 * DIFF_HEAD_DIM)
    m0_ref[...] = _score_sweep(q_ref[0:tq, map0], k_ref, map0, s0_ref, tk)

    def tile(i, carry):
        rows = pl.ds(pl.multiple_of(i * tq, tq), tq)
        m1, acc, l = _fused_sweep(q_ref[rows, map1], k_ref, map1, s1_ref, s0_ref, m0_ref[...], v_ref, tk, True)
        part0 = acc / l
        nxt = pl.ds(pl.multiple_of(jnp.minimum(i + 1, ntile - 1) * tq, tq), tq)
        m0, acc, l = _fused_sweep(q_ref[nxt, map0], k_ref, map0, s0_ref, s1_ref, m1, v_ref, tk, True)
        m0_ref[...] = m0
        o = part0 - lam_full * (acc / l)
        o = _rms(o, gsub_ref[...], 2 * DIFF_HEAD_DIM) * (1.0 - lambda_init)
        o_ref[rows, :] = o.astype(o_ref.dtype)
        return carry

    lax.fori_loop(0, ntile, tile, 0)


def _diff_attention(qk, v, lam, g_sub, lambda_init, batch, seq):
    tq, tk = ATT_TQ, ATT_TK
    hw = 2 * DIFF_HEAD_DIM
    blocks = [((seq, hw), BF16)] * 4
    scratch = [((seq // tk, tq, tk), F32)] * 2 + [((tq, 1), F32)]
    return pl.pallas_call(
        functools.partial(_diff_attn_kernel, tq=tq, tk=tk, lambda_init=lambda_init),
        grid=(batch, DIFF_HEADS),
        in_specs=[pl.BlockSpec((4, DIFF_HEAD_DIM), lambda b, h: (0, 0)),
                  pl.BlockSpec((1, hw), lambda b, h: (0, 0)),
                  pl.BlockSpec((seq, hw), lambda b, h: (b, h)),
                  pl.BlockSpec((seq, hw), lambda b, h: (b, DIFF_HEADS + h)),
                  pl.BlockSpec((seq, hw), lambda b, h: (b, h))],
        out_specs=pl.BlockSpec((seq, hw), lambda b, h: (b, h)),
        out_shape=jax.ShapeDtypeStruct((batch * seq, DIFF_V_WIDTH), BF16),
        scratch_shapes=[pltpu.VMEM(s, d) for s, d in scratch],
        compiler_params=pltpu.CompilerParams(
            dimension_semantics=("arbitrary", "arbitrary"),
            vmem_limit_bytes=_vmem_limit(blocks, scratch)),
        name="diff_attention",
    )(lam, g_sub[None, :], qk, qk, v)


def _mla_attn_kernel(q_ref, k_ref, v_ref, o_ref, s0_ref, s1_ref, m0_ref, *, tq, tk):
    npair = q_ref.shape[0] // (2 * tq)
    m0_ref[...] = _score_sweep(q_ref[0:tq, :], k_ref, slice(None), s0_ref, tk)

    def finish(acc, rows):
        o_ref[rows, :] = (acc[:, :MLA_V] / acc[:, MLA_V:MLA_V + 1]).astype(o_ref.dtype)

    def pair(i, carry):
        even = pl.ds(pl.multiple_of(2 * i * tq, tq), tq)
        odd = pl.ds(pl.multiple_of((2 * i + 1) * tq, tq), tq)
        m1, acc, _ = _fused_sweep(q_ref[odd, :], k_ref, slice(None), s1_ref, s0_ref, m0_ref[...], v_ref, tk, False)
        finish(acc, even)
        nxt = pl.ds(pl.multiple_of(2 * jnp.minimum(i + 1, npair - 1) * tq, tq), tq)
        m0, acc, _ = _fused_sweep(q_ref[nxt, :], k_ref, slice(None), s0_ref, s1_ref, m1, v_ref, tk, False)
        m0_ref[...] = m0
        finish(acc, odd)
        return carry

    lax.fori_loop(0, npair, pair, 0)


def _mla_attention(q, k, v, batch, seq):
    tq, tk = ATT_TQ, ATT_TK
    blocks = [((seq, MLA_QK_PAD), BF16)] * 2 + [((seq, MLA_V_PAD), BF16), ((seq, MLA_V), BF16)]
    scratch = [((seq // tk, tq, tk), F32)] * 2 + [((tq, 1), F32)]
    return pl.pallas_call(
        functools.partial(_mla_attn_kernel, tq=tq, tk=tk),
        grid=(batch, MLA_HEADS),
        in_specs=[pl.BlockSpec((seq, MLA_QK_PAD), lambda b, h: (b, h)),
                  pl.BlockSpec((seq, MLA_QK_PAD), lambda b, h: (b, h)),
                  pl.BlockSpec((seq, MLA_V_PAD), lambda b, h: (b, h))],
        out_specs=pl.BlockSpec((seq, MLA_V), lambda b, h: (b, h)),
        out_shape=jax.ShapeDtypeStruct((batch * seq, MLA_HEADS * MLA_V), BF16),
        scratch_shapes=[pltpu.VMEM(s, d) for s, d in scratch],
        compiler_params=pltpu.CompilerParams(
            dimension_semantics=("arbitrary", "arbitrary"),
            vmem_limit_bytes=_vmem_limit(blocks, scratch)),
        name="mla_attention",
    )(q, k, v)


def _row_spec(cols):
    return (TM, cols), (lambda i, j, k: (i, 0))


def _lane_gain(g):
    return jnp.broadcast_to(g[:, None], (g.shape[0], LANES))


def _diff_mixer(stream, g_mix, w_in, w_out, layer, g_q, g_k, lam, g_sub, tabs, lambda_init, batch, seq):
    x, xb, ssq = stream
    qscale = DIFF_HEAD_DIM ** -0.5 * LOG2E
    gain = jnp.concatenate([jnp.tile(g_q * qscale, 2 * DIFF_HEADS), jnp.tile(g_k, 2 * DIFF_HEADS)])[None, :]
    tab_extras = [(t, *_row_spec(LANES)) for t in tabs]
    g_rows = _lane_gain(g_mix)
    (qk,) = _matmul(xb, w_in, layer=layer, cols=(0, 2 * DIFF_QK_WIDTH), tm=TM, tn=TN, tk=D_MODEL,
                    norm=ssq, gain=g_rows, epilogue=_ep_diff_qk,
                    extras=[(gain, (1, TN), lambda i, j, k: (0, j))] + tab_extras,
                    outs=[(BF16, 2 * DIFF_QK_WIDTH, TN)], name="diff_qk_proj")
    (v,) = _matmul(xb, w_in, layer=layer, cols=(2 * DIFF_QK_WIDTH, DIFF_V_WIDTH), tm=TM, tn=TN, tk=D_MODEL,
                   norm=ssq, gain=g_rows, epilogue=_ep_store, outs=[(BF16, DIFF_V_WIDTH, TN)],
                   name="diff_v_proj")
    o = _diff_attention(qk, v, lam, g_sub, lambda_init, batch, seq)
    return _matmul(o, w_out, layer=layer, tm=TM, tn=TN, tk=DIFF_V_WIDTH, epilogue=_ep_resid, stats=True,
                   extras=[(x, (TM, TN), lambda i, j, k: (i, j))],
                   outs=[(F32, D_MODEL, TN)], name="diff_out_proj")


def _mla_mixer(stream, g_mix, w_in, g_cq, g_ckv, w_uq, w_ukv, g_q, g_k, w_out, layer, tabs, batch, seq):
    x, xb, ssq = stream
    tab_extras = [(t, *_row_spec(LANES)) for t in tabs]
    zpad = jnp.zeros((MLA_ROPE,), F32)
    n_in = MLA_Q_RANK + MLA_KV_RANK + LANES
    w_in_p = jnp.pad(w_in * g_mix[:, None], ((0, 0), (0, n_in - w_in.shape[1]))).astype(BF16)
    const = lambda i, j, k: (0, 0)
    tm_in = 512
    cq, ckv, kr = _matmul(
        xb, w_in_p, tm=tm_in, tn=n_in, tk=D_MODEL, norm=ssq, epilogue=_ep_mla_in,
        extras=[(g_cq[None, :], (1, MLA_Q_RANK), const), (g_ckv[None, :], (1, MLA_KV_RANK), const),
                (jnp.concatenate([g_k[MLA_NOPE:], zpad])[None, :], (1, LANES), const)]
               + [(t, (tm_in, LANES), lambda i, j, k: (i, 0)) for t in tabs],
        outs=[(BF16, MLA_Q_RANK, MLA_Q_RANK), (BF16, MLA_KV_RANK, MLA_KV_RANK), (BF16, LANES, LANES)],
        name="mla_in_proj")
    w_uq_h = w_uq.reshape(MLA_Q_RANK, MLA_HEADS, MLA_QK)
    w_uq_p = jnp.pad(w_uq_h, ((0, 0), (0, 0), (0, MLA_QK_PAD - MLA_QK))).reshape(
        MLA_Q_RANK, MLA_HEADS * MLA_QK_PAD).astype(BF16)
    qscale = MLA_QK ** -0.5 * LOG2E
    g_qs = g_q * qscale
    (q,) = _matmul(
        cq, w_uq_p, tm=TM, tn=TN, tk=MLA_Q_RANK, epilogue=_ep_mla_q,
        extras=[(g_qs[None, :MLA_NOPE], (1, LANES), const),
                (jnp.concatenate([g_qs[MLA_NOPE:], zpad])[None, :], (1, LANES), const)] + tab_extras,
        outs=[(BF16, MLA_HEADS * MLA_QK_PAD, TN)], name="mla_q_proj")
    w_ukv_h = w_ukv.reshape(MLA_KV_RANK, MLA_HEADS, MLA_NOPE + MLA_V).astype(BF16)
    w_uk = w_ukv_h[:, :, :MLA_NOPE].reshape(MLA_KV_RANK, MLA_HEADS * MLA_NOPE)
    w_uv = w_ukv_h[:, :, MLA_NOPE:].reshape(MLA_KV_RANK, MLA_HEADS * MLA_V)
    k, v = _matmul(
        ckv, w_uk, tm=TM, tn=TN, tk=MLA_KV_RANK, epilogue=_ep_mla_kv,
        extras=[(g_k[None, :MLA_NOPE], (1, LANES), const),
                (kr, (TM, LANES), lambda i, j, k: (i, 0)),
                (ckv, (TM, MLA_KV_RANK), lambda i, j, k: (i, 0)),
                (w_uv, (MLA_KV_RANK, TN), lambda i, j, k: (0, j))],
        outs=[(BF16, MLA_HEADS * MLA_QK_PAD, 2 * TN), (BF16, MLA_HEADS * MLA_V_PAD, 2 * TN)],
        name="mla_kv_proj")
    o = _mla_attention(q, k, v, batch, seq)
    return _matmul(o, w_out, layer=layer, tm=TM, tn=TN, tk=MLA_HEADS * MLA_V, epilogue=_ep_resid, stats=True,
                   extras=[(x, (TM, TN), lambda i, j, k: (i, j))],
                   outs=[(F32, D_MODEL, TN)], name="mla_out_proj")


def kernel(x, p, positions, g_mix, g_mlp, g_ple, w1, w2, w_gate, w_ple, diff_w_in, diff_w_out, diff_g_q, diff_g_k, diff_lambda, diff_g_sub, mla_w_in, mla_g_cq, mla_g_ckv, mla_w_uq, mla_w_ukv, mla_g_q, mla_g_k, mla_w_out):
    batch, seq, d = x.shape
    t = batch * seq
    x = x.reshape(t, d)
    pos = positions.reshape(t, 1).astype(F32)
    tabs_d = _rope_tables(pos, DIFF_ROT, keep_rest=True)
    tabs_m = _rope_tables(pos, MLA_ROPE, keep_rest=False)
    p_bf = p.reshape(DEPTH, t, PLE_DIM).astype(BF16)
    w2_bf = w2.astype(BF16)
    stream = (x, *_stream_stats(x))
    for i in range(DEPTH):
        j = i // N_MIXERS
        if i % N_MIXERS == 0:
            lambda_init = 0.8 - 0.6 * math.exp(-0.3 * i)
            stream = _diff_mixer(stream, g_mix[i], diff_w_in, diff_w_out, j, diff_g_q[j], diff_g_k[j],
                                 diff_lambda[j], diff_g_sub[j], tabs_d, lambda_init, batch, seq)
        else:
            stream = _mla_mixer(stream, g_mix[i], mla_w_in[j], mla_g_cq[j], mla_g_ckv[j], mla_w_uq[j],
                                mla_w_ukv[j], mla_g_q[j], mla_g_k[j], mla_w_out, j, tabs_m, batch, seq)
        x, xb, ssq = stream
        (a,) = _matmul(xb, w1, layer=i, tm=TM, tn=TN, tk=D_MODEL, norm=ssq, gain=_lane_gain(g_mlp[i]),
                       epilogue=_ep_relu2, outs=[(BF16, D_FF, TN)], name="mlp_up")
        x, xb, ssq = _matmul(a, w2_bf, layer=i, tm=TM, tn=2 * TN, tk=2048, epilogue=_ep_resid, stats=True,
                             extras=[(x, (TM, 2 * TN), lambda i_, j_, k_: (i_, j_))],
                             outs=[(F32, D_MODEL, 2 * TN)], name="mlp_down")
        stream = _matmul(xb, w_gate, layer=i, tm=TM, tn=TN, tk=D_MODEL, norm=ssq, gain=_lane_gain(g_ple[i]),
                         epilogue=_ep_gate, stats=i + 1 < DEPTH,
                         extras=[(x, (TM, TN), lambda i_, j_, k_: (i_, j_)),
                                 (p_bf[i], (TM, PLE_DIM), lambda i_, j_, k_: (i_, 0)),
                                 (w_ple[i].astype(BF16), (PLE_DIM, TN), lambda i_, j_, k_: (0, j_))],
                         outs=[(F32, D_MODEL, TN)], name="ple_gate")
    return stream[0].reshape(batch, seq, d)
```

```python
import functools
import math

import jax
import jax.numpy as jnp
from jax import lax
from jax.experimental import pallas as pl
from jax.experimental.pallas import tpu as pltpu

D_MODEL = 4096
DEPTH = 4
N_MIXERS = 2
ROPE_THETA = 500000.0
EPS = 1e-6
D_FF = 4 * D_MODEL
PLE_DIM = 256

DIFF_HEAD_DIM = 128
DIFF_HEADS = D_MODEL // (2 * DIFF_HEAD_DIM)
DIFF_ROT = DIFF_HEAD_DIM // 4
DIFF_QK_WIDTH = DIFF_HEADS * 2 * DIFF_HEAD_DIM
DIFF_V_WIDTH = DIFF_HEADS * 2 * DIFF_HEAD_DIM

MLA_HEADS = 32
MLA_NOPE = 128
MLA_ROPE = 64
MLA_V = 128
MLA_QK = MLA_NOPE + MLA_ROPE
MLA_Q_RANK = 1024
MLA_KV_RANK = 512

LANES = 128
MLA_QK_PAD = 2 * LANES
V7X_VMEM_BYTES = 64 * 1024 * 1024
VMEM_TEMP_ALLOWANCE = 12 * 1024 * 1024
LOG2E = math.log2(math.e)

TM = 1024
TN = 512
SUB_ROWS = 256
ATT_TQ = 512
DIFF_TQ = 256
ATT_TK = 1024

F32 = jnp.float32
BF16 = jnp.bfloat16


def _nbytes(shape, dtype):
    return math.prod(shape) * jnp.dtype(dtype).itemsize


def _vmem_limit(blocks, scratch=()):
    need = 2 * sum(_nbytes(s, d) for s, d in blocks) + sum(_nbytes(s, d) for s, d in scratch)
    return min(need + VMEM_TEMP_ALLOWANCE, V7X_VMEM_BYTES - 4 * 1024 * 1024)


def _rope_table_kernel(pos_ref, inv_ref, c_ref, s1_ref, s2_ref, *, half, keep_rest):
    ang = pos_ref[...] * inv_ref[...]
    lane = lax.broadcasted_iota(jnp.int32, ang.shape, 1)
    c = jnp.cos(ang)
    s = jnp.sin(ang)
    rest = 1.0 if keep_rest else 0.0
    c_ref[...] = jnp.where(lane < 2 * half, c, rest)
    s1_ref[...] = jnp.where((lane >= half) & (lane < 2 * half), s, 0.0)
    s2_ref[...] = jnp.where(lane < half, -s, 0.0)


def _rope_tables(pos_f32, rot_dim, keep_rest):
    t = pos_f32.shape[0]
    half = rot_dim // 2
    inv = ROPE_THETA ** (-jnp.arange(0, rot_dim, 2, dtype=F32) / rot_dim)
    inv_row = jnp.concatenate([inv, inv, jnp.zeros((LANES - rot_dim,), F32)])[None, :]
    tm = 1024
    out = jax.ShapeDtypeStruct((t, LANES), F32)
    spec = pl.BlockSpec((tm, LANES), lambda i: (i, 0))
    return pl.pallas_call(
        functools.partial(_rope_table_kernel, half=half, keep_rest=keep_rest),
        grid=(t // tm,),
        in_specs=[pl.BlockSpec((tm, 1), lambda i: (i, 0)),
                  pl.BlockSpec((1, LANES), lambda i: (0, 0))],
        out_specs=[spec, spec, spec],
        out_shape=[out, out, out],
        name="rope_tables",
    )(pos_f32, inv_row)


def _rope(y, c, s1, s2, half):
    return y * c + pltpu.roll(y, half, 1) * s1 + pltpu.roll(y, LANES - half, 1) * s2


def _rms(x, gain, n):
    ms = jnp.sum(x * x, axis=-1, keepdims=True) * (1.0 / n)
    return x * lax.rsqrt(ms + EPS) * gain


def _stream_stats_kernel(x_ref, xb_ref, ssq_ref):
    x = x_ref[...]
    xb_ref[...] = x.astype(BF16)
    sq = x * x
    part = sq[:, :LANES]
    for c in range(1, sq.shape[1] // LANES):
        part = part + sq[:, c * LANES:(c + 1) * LANES]
    ssq_ref[...] = part


def _stream_stats(x):
    t, d = x.shape
    tm = 256
    return pl.pallas_call(
        _stream_stats_kernel,
        grid=(t // tm,),
        in_specs=[pl.BlockSpec((tm, d), lambda i: (i, 0))],
        out_specs=[pl.BlockSpec((tm, d), lambda i: (i, 0)), pl.BlockSpec((tm, LANES), lambda i: (i, 0))],
        out_shape=[jax.ShapeDtypeStruct((t, d), BF16), jax.ShapeDtypeStruct((t, LANES), F32)],
        compiler_params=pltpu.CompilerParams(
            dimension_semantics=("arbitrary",),
            vmem_limit_bytes=_vmem_limit([((tm, d), F32), ((tm, d), BF16), ((tm, LANES), F32)])),
        name="stream_stats",
    )(x)


def _mm_body(*refs, nk, n_extra, n_out, epilogue, sub, has_norm, has_gain, stats):
    a_ref, w_ref = refs[0], refs[1]
    pos = 2
    ssq_ref = refs[pos] if has_norm else None
    pos += has_norm
    gain_ref = refs[pos] if has_gain else None
    pos += has_gain
    extra = refs[pos:pos + n_extra]
    outs = refs[pos + n_extra:pos + n_extra + n_out]
    tm = a_ref.shape[0]

    if stats:
        @pl.when((pl.program_id(1) == 0) & (pl.program_id(2) == 0))
        def _():
            outs[-1][...] = jnp.zeros_like(outs[-1])

    def weights():
        w = w_ref[...]
        if has_gain:
            g = gain_ref[...]
            w = jnp.concatenate([w[:, c * LANES:(c + 1) * LANES] * g for c in range(w.shape[1] // LANES)], axis=1)
        return w.astype(BF16)

    def scaled(acc, rows):
        if not has_norm:
            return acc
        ms = jnp.sum(ssq_ref[rows, :], axis=-1, keepdims=True) * (1.0 / D_MODEL)
        return acc * lax.rsqrt(ms + EPS)

    if nk == 1:
        w = weights()
        for r in range(tm // sub):
            rows = slice(r * sub, (r + 1) * sub)
            acc = jnp.dot(a_ref[rows, :], w, preferred_element_type=F32)
            epilogue(scaled(acc, rows), extra, outs, rows)
        return
    assert not has_norm
    acc_ref = refs[-1]
    k = pl.program_id(2)

    def partial_dots():
        w = weights()
        for r in range(tm // sub):
            rows = slice(r * sub, (r + 1) * sub)
            yield rows, jnp.dot(a_ref[rows, :], w, preferred_element_type=F32)

    @pl.when(k == 0)
    def _():
        for rows, d in partial_dots():
            acc_ref[rows, :] = d

    @pl.when((k > 0) & (k < nk - 1))
    def _():
        for rows, d in partial_dots():
            acc_ref[rows, :] += d

    @pl.when(k == nk - 1)
    def _():
        for rows, d in partial_dots():
            epilogue(acc_ref[rows, :] + d, extra, outs, rows)


def _matmul(a, w, *, tm, tn, tk, epilogue, extras=(), outs, name, sub=SUB_ROWS, layer=None, cols=None,
            norm=None, gain=None, stats=False):
    m, kdim = a.shape
    col0, n = cols if cols is not None else (0, w.shape[-1])
    nk = kdim // tk
    assert m % tm == 0 and n % tn == 0 and col0 % tn == 0 and kdim % tk == 0 and tm % min(sub, tm) == 0
    j0 = col0 // tn
    if layer is None:
        w_spec = pl.BlockSpec((tk, tn), lambda i, j, k: (k, j0 + j))
    else:
        w_spec = pl.BlockSpec((None, tk, tn), lambda i, j, k: (layer, k, j0 + j))
    in_specs = [pl.BlockSpec((tm, tk), lambda i, j, k: (i, k)), w_spec]
    blocks = [((tm, tk), a.dtype), ((tk, tn), w.dtype)]
    operands = [a, w]
    if norm is not None:
        extras = [(norm, (tm, LANES), lambda i, j, k: (i, 0))] + list(extras)
    if gain is not None:
        pos = 1 if norm is not None else 0
        extras = list(extras[:pos]) + [(gain, (tk, LANES), lambda i, j, k: (k, 0))] + list(extras[pos:])
    for arr, bshape, imap in extras:
        in_specs.append(pl.BlockSpec(bshape, imap))
        blocks.append((bshape, arr.dtype))
        operands.append(arr)
    out_specs, out_shape = [], []
    for dtype, ocols, bcols in outs:
        out_specs.append(pl.BlockSpec((tm, bcols), lambda i, j, k: (i, j)))
        out_shape.append(jax.ShapeDtypeStruct((m, ocols), dtype))
        blocks.append(((tm, bcols), dtype))
    if stats:
        out_specs.append(pl.BlockSpec((tm, tn), lambda i, j, k: (i, j)))
        out_shape.append(jax.ShapeDtypeStruct((m, n), BF16))
        out_specs.append(pl.BlockSpec((tm, LANES), lambda i, j, k: (i, 0)))
        out_shape.append(jax.ShapeDtypeStruct((m, LANES), F32))
        blocks += [((tm, tn), BF16), ((tm, LANES), F32)]
    scratch = [((tm, tn), F32)] if nk > 1 else []
    n_body = (norm is not None) + (gain is not None)
    return pl.pallas_call(
        functools.partial(_mm_body, nk=nk, n_extra=len(extras) - n_body, n_out=len(out_specs),
                          epilogue=epilogue, sub=min(sub, tm), has_norm=norm is not None,
                          has_gain=gain is not None, stats=stats),
        grid=(m // tm, n // tn, nk),
        in_specs=in_specs,
        out_specs=out_specs,
        out_shape=out_shape,
        scratch_shapes=[pltpu.VMEM(s, d) for s, d in scratch],
        compiler_params=pltpu.CompilerParams(
            dimension_semantics=("arbitrary", "arbitrary", "arbitrary"),
            vmem_limit_bytes=_vmem_limit(blocks, scratch)),
        name=name,
    )(*operands)


def _emit_stream(x_new, outs, rows):
    outs[0][rows, :] = x_new
    if len(outs) == 3:
        outs[1][rows, :] = x_new.astype(BF16)
        sq = x_new * x_new
        part = sq[:, :LANES]
        for c in range(1, sq.shape[1] // LANES):
            part = part + sq[:, c * LANES:(c + 1) * LANES]
        outs[2][rows, :] += part


def _ep_store(acc, extra, outs, rows):
    outs[0][rows, :] = acc.astype(outs[0].dtype)


def _ep_resid(acc, extra, outs, rows):
    _emit_stream(extra[0][rows, :] + acc, outs, rows)


def _ep_relu2(acc, extra, outs, rows):
    r = jnp.maximum(acc, 0.0)
    outs[0][rows, :] = (r * r).astype(outs[0].dtype)


def _ep_gate(acc, extra, outs, rows):
    res_ref, p_ref, wple_ref = extra
    ple = jnp.dot(p_ref[rows, :], wple_ref[...], preferred_element_type=F32)
    gate = 1.0 / (1.0 + jnp.exp(-acc))
    _emit_stream(res_ref[rows, :] + gate * ple, outs, rows)


def _ep_diff_qk(acc, extra, outs, rows):
    gain_ref, c_ref, s1_ref, s2_ref = extra
    c, s1, s2 = c_ref[rows, :], s1_ref[rows, :], s2_ref[rows, :]
    for g in range(acc.shape[1] // LANES):
        sl = slice(g * LANES, (g + 1) * LANES)
        y = _rms(acc[:, sl], gain_ref[:, sl], DIFF_HEAD_DIM)
        outs[0][rows, sl] = _rope(y, c, s1, s2, DIFF_ROT // 2).astype(outs[0].dtype)


def _ep_mla_in(acc, extra, outs, rows):
    gcq_ref, gckv_ref, gkr_ref, c_ref, s1_ref, s2_ref = extra
    cq_ref, ckv_ref, kr_ref = outs
    cq_ref[rows, :] = _rms(acc[:, :MLA_Q_RANK], gcq_ref[...], MLA_Q_RANK).astype(cq_ref.dtype)
    lo = MLA_Q_RANK
    ckv_ref[rows, :] = _rms(acc[:, lo:lo + MLA_KV_RANK], gckv_ref[...], MLA_KV_RANK).astype(ckv_ref.dtype)
    lo += MLA_KV_RANK
    y = _rms(acc[:, lo:lo + LANES], gkr_ref[...], MLA_ROPE)
    kr_ref[rows, :] = _rope(y, c_ref[rows, :], s1_ref[rows, :], s2_ref[rows, :],
                            MLA_ROPE // 2).astype(kr_ref.dtype)


def _ep_mla_q(acc, extra, outs, rows):
    gn_ref, gr_ref, c_ref, s1_ref, s2_ref = extra
    c, s1, s2 = c_ref[rows, :], s1_ref[rows, :], s2_ref[rows, :]
    for h in range(acc.shape[1] // MLA_QK_PAD):
        lo = h * MLA_QK_PAD
        yn = _rms(acc[:, lo:lo + LANES], gn_ref[...], MLA_NOPE)
        outs[0][rows, lo:lo + LANES] = yn.astype(outs[0].dtype)
        yr = _rms(acc[:, lo + LANES:lo + 2 * LANES], gr_ref[...], MLA_ROPE)
        outs[0][rows, lo + LANES:lo + 2 * LANES] = _rope(yr, c, s1, s2, MLA_ROPE // 2).astype(outs[0].dtype)


def _ep_mla_kv(acc, extra, outs, rows):
    gk_ref, kr_ref, a_ref, wuv_ref = extra
    k_ref, v_ref = outs
    kr = kr_ref[rows, :]
    v_ref[rows, :] = jnp.dot(a_ref[rows, :], wuv_ref[...], preferred_element_type=F32).astype(v_ref.dtype)
    for h in range(acc.shape[1] // LANES):
        yk = _rms(acc[:, h * LANES:(h + 1) * LANES], gk_ref[...], MLA_NOPE)
        lo = h * MLA_QK_PAD
        k_ref[rows, lo:lo + LANES] = yk.astype(k_ref.dtype)
        k_ref[rows, lo + LANES:lo + 2 * LANES] = kr


SUBLANE_PARTIALS = 32


def _fold_rows(x, op):
    r = x.shape[0] // SUBLANE_PARTIALS
    return op(x.reshape(r, SUBLANE_PARTIALS, x.shape[1]), axis=0)


def _score_sweep_t(q, k_ref, k_cols, s_ref, tk):
    mrun = None
    for j in range(s_ref.shape[0]):
        s = lax.dot_general(k_ref[j * tk:(j + 1) * tk, k_cols], q, (((1,), (1,)), ((), ())),
                            preferred_element_type=F32)
        s_ref[j] = s
        part = _fold_rows(s, jnp.max)
        mrun = part if mrun is None else jnp.maximum(mrun, part)
    return jnp.max(mrun, axis=0, keepdims=True)


def _fused_sweep_t(q_next, k_ref, k_cols, s_next_ref, s_ref, m, vt_ref, tk, want_rowsum):
    mrun, acc, lrun = None, None, None
    for j in range(s_ref.shape[0]):
        s = lax.dot_general(k_ref[j * tk:(j + 1) * tk, k_cols], q_next, (((1,), (1,)), ((), ())),
                            preferred_element_type=F32)
        s_next_ref[j] = s
        e = jnp.exp2(s_ref[j] - m)
        part = _fold_rows(s, jnp.max)
        mrun = part if mrun is None else jnp.maximum(mrun, part)
        if want_rowsum:
            part = _fold_rows(e, jnp.sum)
            lrun = part if lrun is None else lrun + part
        d = jnp.dot(vt_ref[:, j * tk:(j + 1) * tk], e.astype(BF16), preferred_element_type=F32)
        acc = d if acc is None else acc + d
    l = jnp.sum(lrun, axis=0, keepdims=True) if want_rowsum else None
    return jnp.max(mrun, axis=0, keepdims=True), acc, l


def _diff_scores(q_ref, rows, k_ref, bufs, tk):
    ms = []
    for c, buf in enumerate(bufs):
        cols = slice(c * DIFF_HEAD_DIM, (c + 1) * DIFF_HEAD_DIM)
        ms.append(_score_sweep_t(q_ref[rows, cols], k_ref, cols, buf, tk))
    return ms


def _diff_stage(q_ref, rows_next, k_ref, bufs_next, bufs, ms, vt_ref, lam_full, gsub_ref, scale_out, tk):
    nchunk = bufs[0].shape[0]
    q_next = [q_ref[rows_next, c * DIFF_HEAD_DIM:(c + 1) * DIFF_HEAD_DIM] for c in range(2)]
    mrun, lrun = [None, None], [None, None]
    for j in range(nchunk):
        keys = slice(j * tk, (j + 1) * tk)
        for c in range(2):
            cols = slice(c * DIFF_HEAD_DIM, (c + 1) * DIFF_HEAD_DIM)
            s = lax.dot_general(k_ref[keys, cols], q_next[c], (((1,), (1,)), ((), ())),
                                preferred_element_type=F32)
            bufs_next[c][j] = s
            part = _fold_rows(s, jnp.max)
            mrun[c] = part if mrun[c] is None else jnp.maximum(mrun[c], part)
        for c in range(2):
            e = jnp.exp2(bufs[c][j] - ms[c])
            bufs[c][j] = e
            part = _fold_rows(e, jnp.sum)
            lrun[c] = part if lrun[c] is None else lrun[c] + part
    l0 = jnp.sum(lrun[0], axis=0, keepdims=True)
    l1 = jnp.sum(lrun[1], axis=0, keepdims=True)
    rho = lam_full * l0 / l1
    acc = None
    for j in range(nchunk):
        a = bufs[0][j] - bufs[1][j] * rho
        d = jnp.dot(vt_ref[:, j * tk:(j + 1) * tk], a.astype(BF16), preferred_element_type=F32)
        acc = d if acc is None else acc + d
    acc = acc * (1.0 / l0)
    ms_o = jnp.sum(acc * acc, axis=0, keepdims=True) * (1.0 / (2 * DIFF_HEAD_DIM))
    y = acc * lax.rsqrt(ms_o + EPS) * gsub_ref[...] * scale_out
    m_next = [jnp.max(m, axis=0, keepdims=True) for m in mrun]
    return m_next, y.T


def _diff_attn_kernel(lam_ref, gsub_ref, q_ref, k_ref, v_ref, o_ref, a0_ref, a1_ref, b0_ref, b1_ref,
                      m_ref, vt_ref, *, tq, tk, lambda_init):
    lam = lam_ref[...]
    lam_full = (jnp.exp(jnp.sum(lam[0:1] * lam[1:2], axis=-1, keepdims=True))
                - jnp.exp(jnp.sum(lam[2:3] * lam[3:4], axis=-1, keepdims=True)) + lambda_init)
    vt_ref[...] = v_ref[...].T
    npair = q_ref.shape[0] // (2 * tq)
    bufs_a, bufs_b = (a0_ref, a1_ref), (b0_ref, b1_ref)
    m0, m1 = _diff_scores(q_ref, slice(0, tq), k_ref, bufs_a, tk)
    m_ref[0:1, :] = m0
    m_ref[1:2, :] = m1

    def pair(i, carry):
        even = pl.ds(pl.multiple_of(2 * i * tq, tq), tq)
        odd = pl.ds(pl.multiple_of((2 * i + 1) * tq, tq), tq)
        ms_b, y = _diff_stage(q_ref, odd, k_ref, bufs_b, bufs_a, [m_ref[0:1, :], m_ref[1:2, :]], vt_ref,
                              lam_full, gsub_ref, 1.0 - lambda_init, tk)
        o_ref[even, :] = y.astype(o_ref.dtype)
        nxt = pl.ds(pl.multiple_of(2 * jnp.minimum(i + 1, npair - 1) * tq, tq), tq)
        ms_a, y = _diff_stage(q_ref, nxt, k_ref, bufs_a, bufs_b, ms_b, vt_ref,
                              lam_full, gsub_ref, 1.0 - lambda_init, tk)
        o_ref[odd, :] = y.astype(o_ref.dtype)
        m_ref[0:1, :] = ms_a[0]
        m_ref[1:2, :] = ms_a[1]
        return carry

    lax.fori_loop(0, npair, pair, 0)


def _diff_attention(qk, v, lam, g_sub, lambda_init, batch, seq):
    tq, tk = DIFF_TQ, ATT_TK
    hw = 2 * DIFF_HEAD_DIM
    blocks = [((seq, hw), BF16)] * 4 + [((hw, tq), F32)]
    scratch = [((seq // tk, tk, tq), F32)] * 4 + [((2, tq), F32), ((hw, seq), BF16)]
    return pl.pallas_call(
        functools.partial(_diff_attn_kernel, tq=tq, tk=tk, lambda_init=lambda_init),
        grid=(batch, DIFF_HEADS),
        in_specs=[pl.BlockSpec((4, DIFF_HEAD_DIM), lambda b, h: (0, 0)),
                  pl.BlockSpec((hw, tq), lambda b, h: (0, 0)),
                  pl.BlockSpec((seq, hw), lambda b, h: (b, h)),
                  pl.BlockSpec((seq, hw), lambda b, h: (b, DIFF_HEADS + h)),
                  pl.BlockSpec((seq, hw), lambda b, h: (b, h))],
        out_specs=pl.BlockSpec((seq, hw), lambda b, h: (b, h)),
        out_shape=jax.ShapeDtypeStruct((batch * seq, DIFF_V_WIDTH), BF16),
        scratch_shapes=[pltpu.VMEM(s, d) for s, d in scratch],
        compiler_params=pltpu.CompilerParams(
            dimension_semantics=("arbitrary", "arbitrary"),
            vmem_limit_bytes=_vmem_limit(blocks, scratch)),
        name="diff_attention",
    )(lam, jnp.broadcast_to(g_sub[:, None], (hw, tq)), qk, qk, v)


def _mla_attn_kernel(q_ref, k_ref, v_ref, o_ref, s0_ref, s1_ref, m0_ref, vt_ref, *, tq, tk):
    vt_ref[...] = v_ref[...].T
    npair = q_ref.shape[0] // (2 * tq)
    m0_ref[...] = _score_sweep_t(q_ref[0:tq, :], k_ref, slice(None), s0_ref, tk)

    def finish(acc, l, rows):
        o_ref[rows, :] = (acc / l).T.astype(o_ref.dtype)

    def pair(i, carry):
        even = pl.ds(pl.multiple_of(2 * i * tq, tq), tq)
        odd = pl.ds(pl.multiple_of((2 * i + 1) * tq, tq), tq)
        m1, acc, l = _fused_sweep_t(q_ref[odd, :], k_ref, slice(None), s1_ref, s0_ref, m0_ref[...], vt_ref, tk,
                                    True)
        finish(acc, l, even)
        nxt = pl.ds(pl.multiple_of(2 * jnp.minimum(i + 1, npair - 1) * tq, tq), tq)
        m0, acc, l = _fused_sweep_t(q_ref[nxt, :], k_ref, slice(None), s0_ref, s1_ref, m1, vt_ref, tk, True)
        m0_ref[...] = m0
        finish(acc, l, odd)
        return carry

    lax.fori_loop(0, npair, pair, 0)


def _mla_attention(q, k, v, batch, seq):
    tq, tk = ATT_TQ, ATT_TK
    blocks = [((seq, MLA_QK_PAD), BF16)] * 2 + [((seq, MLA_V), BF16)] * 2
    scratch = [((seq // tk, tk, tq), F32)] * 2 + [((1, tq), F32), ((MLA_V, seq), BF16)]
    return pl.pallas_call(
        functools.partial(_mla_attn_kernel, tq=tq, tk=tk),
        grid=(batch, MLA_HEADS),
        in_specs=[pl.BlockSpec((seq, MLA_QK_PAD), lambda b, h: (b, h)),
                  pl.BlockSpec((seq, MLA_QK_PAD), lambda b, h: (b, h)),
                  pl.BlockSpec((seq, MLA_V), lambda b, h: (b, h))],
        out_specs=pl.BlockSpec((seq, MLA_V), lambda b, h: (b, h)),
        out_shape=jax.ShapeDtypeStruct((batch * seq, MLA_HEADS * MLA_V), BF16),
        scratch_shapes=[pltpu.VMEM(s, d) for s, d in scratch],
        compiler_params=pltpu.CompilerParams(
            dimension_semantics=("arbitrary", "arbitrary"),
            vmem_limit_bytes=_vmem_limit(blocks, scratch)),
        name="mla_attention",
    )(q, k, v)


def _row_spec(cols):
    return (TM, cols), (lambda i, j, k: (i, 0))


def _lane_gain(g):
    return jnp.broadcast_to(g[:, None], (g.shape[0], LANES))


def _diff_mixer(stream, g_mix, w_in, w_out, layer, g_q, g_k, lam, g_sub, tabs, lambda_init, batch, seq):
    x, xb, ssq = stream
    qscale = DIFF_HEAD_DIM ** -0.5 * LOG2E
    gain = jnp.concatenate([jnp.tile(g_q * qscale, 2 * DIFF_HEADS), jnp.tile(g_k, 2 * DIFF_HEADS)])[None, :]
    tab_extras = [(t, *_row_spec(LANES)) for t in tabs]
    g_rows = _lane_gain(g_mix)
    (qk,) = _matmul(xb, w_in, layer=layer, cols=(0, 2 * DIFF_QK_WIDTH), tm=TM, tn=TN, tk=D_MODEL,
                    norm=ssq, gain=g_rows, epilogue=_ep_diff_qk,
                    extras=[(gain, (1, TN), lambda i, j, k: (0, j))] + tab_extras,
                    outs=[(BF16, 2 * DIFF_QK_WIDTH, TN)], name="diff_qk_proj")
    (v,) = _matmul(xb, w_in, layer=layer, cols=(2 * DIFF_QK_WIDTH, DIFF_V_WIDTH), tm=TM, tn=TN, tk=D_MODEL,
                   norm=ssq, gain=g_rows, epilogue=_ep_store, outs=[(BF16, DIFF_V_WIDTH, TN)],
                   name="diff_v_proj")
    o = _diff_attention(qk, v, lam, g_sub, lambda_init, batch, seq)
    return _matmul(o, w_out, layer=layer, tm=TM, tn=TN, tk=DIFF_V_WIDTH, epilogue=_ep_resid, stats=True,
                   extras=[(x, (TM, TN), lambda i, j, k: (i, j))],
                   outs=[(F32, D_MODEL, TN)], name="diff_out_proj")


def _mla_mixer(stream, g_mix, w_in, g_cq, g_ckv, w_uq, w_ukv, g_q, g_k, w_out, layer, tabs, batch, seq):
    x, xb, ssq = stream
    tab_extras = [(t, *_row_spec(LANES)) for t in tabs]
    zpad = jnp.zeros((MLA_ROPE,), F32)
    n_in = MLA_Q_RANK + MLA_KV_RANK + LANES
    w_in_p = jnp.pad(w_in * g_mix[:, None], ((0, 0), (0, n_in - w_in.shape[1]))).astype(BF16)
    const = lambda i, j, k: (0, 0)
    tm_in = 512
    cq, ckv, kr = _matmul(
        xb, w_in_p, tm=tm_in, tn=n_in, tk=D_MODEL, norm=ssq, epilogue=_ep_mla_in,
        extras=[(g_cq[None, :], (1, MLA_Q_RANK), const), (g_ckv[None, :], (1, MLA_KV_RANK), const),
                (jnp.concatenate([g_k[MLA_NOPE:], zpad])[None, :], (1, LANES), const)]
               + [(t, (tm_in, LANES), lambda i, j, k: (i, 0)) for t in tabs],
        outs=[(BF16, MLA_Q_RANK, MLA_Q_RANK), (BF16, MLA_KV_RANK, MLA_KV_RANK), (BF16, LANES, LANES)],
        name="mla_in_proj")
    w_uq_h = w_uq.reshape(MLA_Q_RANK, MLA_HEADS, MLA_QK)
    w_uq_p = jnp.pad(w_uq_h, ((0, 0), (0, 0), (0, MLA_QK_PAD - MLA_QK))).reshape(
        MLA_Q_RANK, MLA_HEADS * MLA_QK_PAD).astype(BF16)
    qscale = MLA_QK ** -0.5 * LOG2E
    g_qs = g_q * qscale
    (q,) = _matmul(
        cq, w_uq_p, tm=TM, tn=TN, tk=MLA_Q_RANK, epilogue=_ep_mla_q,
        extras=[(g_qs[None, :MLA_NOPE], (1, LANES), const),
                (jnp.concatenate([g_qs[MLA_NOPE:], zpad])[None, :], (1, LANES), const)] + tab_extras,
        outs=[(BF16, MLA_HEADS * MLA_QK_PAD, TN)], name="mla_q_proj")
    w_ukv_h = w_ukv.reshape(MLA_KV_RANK, MLA_HEADS, MLA_NOPE + MLA_V).astype(BF16)
    w_uk = w_ukv_h[:, :, :MLA_NOPE].reshape(MLA_KV_RANK, MLA_HEADS * MLA_NOPE)
    w_uv = w_ukv_h[:, :, MLA_NOPE:].reshape(MLA_KV_RANK, MLA_HEADS * MLA_V)
    k, v = _matmul(
        ckv, w_uk, tm=TM, tn=TN, tk=MLA_KV_RANK, epilogue=_ep_mla_kv,
        extras=[(g_k[None, :MLA_NOPE], (1, LANES), const),
                (kr, (TM, LANES), lambda i, j, k: (i, 0)),
                (ckv, (TM, MLA_KV_RANK), lambda i, j, k: (i, 0)),
                (w_uv, (MLA_KV_RANK, TN), lambda i, j, k: (0, j))],
        outs=[(BF16, MLA_HEADS * MLA_QK_PAD, 2 * TN), (BF16, MLA_HEADS * MLA_V, TN)],
        name="mla_kv_proj")
    o = _mla_attention(q, k, v, batch, seq)
    return _matmul(o, w_out, layer=layer, tm=TM, tn=TN, tk=MLA_HEADS * MLA_V, epilogue=_ep_resid, stats=True,
                   extras=[(x, (TM, TN), lambda i, j, k: (i, j))],
                   outs=[(F32, D_MODEL, TN)], name="mla_out_proj")


def kernel(x, p, positions, g_mix, g_mlp, g_ple, w1, w2, w_gate, w_ple, diff_w_in, diff_w_out, diff_g_q, diff_g_k, diff_lambda, diff_g_sub, mla_w_in, mla_g_cq, mla_g_ckv, mla_w_uq, mla_w_ukv, mla_g_q, mla_g_k, mla_w_out):
    batch, seq, d = x.shape
    t = batch * seq
    x = x.reshape(t, d)
    pos = positions.reshape(t, 1).astype(F32)
    tabs_d = _rope_tables(pos, DIFF_ROT, keep_rest=True)
    tabs_m = _rope_tables(pos, MLA_ROPE, keep_rest=False)
    p_bf = p.reshape(DEPTH, t, PLE_DIM).astype(BF16)
    stream = (x, *_stream_stats(x))
    for i in range(DEPTH):
        j = i // N_MIXERS
        if i % N_MIXERS == 0:
            lambda_init = 0.8 - 0.6 * math.exp(-0.3 * i)
            stream = _diff_mixer(stream, g_mix[i], diff_w_in, diff_w_out, j, diff_g_q[j], diff_g_k[j],
                                 diff_lambda[j], diff_g_sub[j], tabs_d, lambda_init, batch, seq)
        else:
            stream = _mla_mixer(stream, g_mix[i], mla_w_in[j], mla_g_cq[j], mla_g_ckv[j], mla_w_uq[j],
                                mla_w_ukv[j], mla_g_q[j], mla_g_k[j], mla_w_out, j, tabs_m, batch, seq)
        x, xb, ssq = stream
        (a,) = _matmul(xb, w1, layer=i, tm=TM, tn=TN, tk=D_MODEL, norm=ssq, gain=_lane_gain(g_mlp[i]),
                       epilogue=_ep_relu2, outs=[(BF16, D_FF, TN)], name="mlp_up")
        x, xb, ssq = _matmul(a, w2, layer=i, tm=TM, tn=2 * TN, tk=2048, epilogue=_ep_resid, stats=True,
                             extras=[(x, (TM, 2 * TN), lambda i_, j_, k_: (i_, j_))],
                             outs=[(F32, D_MODEL, 2 * TN)], name="mlp_down")
        stream = _matmul(xb, w_gate, layer=i, tm=TM, tn=TN, tk=D_MODEL, norm=ssq, gain=_lane_gain(g_ple[i]),
                         epilogue=_ep_gate, stats=i + 1 < DEPTH,
                         extras=[(x, (TM, TN), lambda i_, j_, k_: (i_, j_)),
                                 (p_bf[i], (TM, PLE_DIM), lambda i_, j_, k_: (i_, 0)),
                                 (w_ple[i].astype(BF16), (PLE_DIM, TN), lambda i_, j_, k_: (0, j_))],
                         outs=[(F32, D_MODEL, TN)], name="ple_gate")
    return stream[0].reshape(batch, seq, d)
```

```python
import functools
import math

import jax
import jax.numpy as jnp
from jax import lax
from jax.experimental import pallas as pl
from jax.experimental.pallas import tpu as pltpu

D_MODEL = 4096
DEPTH = 4
N_MIXERS = 2
ROPE_THETA = 500000.0
EPS = 1e-6
D_FF = 4 * D_MODEL
PLE_DIM = 256

DIFF_HEAD_DIM = 128
DIFF_HEADS = D_MODEL // (2 * DIFF_HEAD_DIM)
DIFF_ROT = DIFF_HEAD_DIM // 4
DIFF_QK_WIDTH = DIFF_HEADS * 2 * DIFF_HEAD_DIM
DIFF_V_WIDTH = DIFF_HEADS * 2 * DIFF_HEAD_DIM

MLA_HEADS = 32
MLA_NOPE = 128
MLA_ROPE = 64
MLA_V = 128
MLA_QK = MLA_NOPE + MLA_ROPE
MLA_Q_RANK = 1024
MLA_KV_RANK = 512

LANES = 128
MLA_QK_PAD = 2 * LANES
MLA_V_PAD = 2 * LANES
V7X_VMEM_BYTES = 64 * 1024 * 1024
VMEM_TEMP_ALLOWANCE = 12 * 1024 * 1024
LOG2E = math.log2(math.e)

TM = 1024
TN = 512
SUB_ROWS = 256
ATT_TQ = 512
DIFF_TQ = 256
ATT_TK = 1024
MLA_TK = 512

F32 = jnp.float32
BF16 = jnp.bfloat16


def _nbytes(shape, dtype):
    return math.prod(shape) * jnp.dtype(dtype).itemsize


def _vmem_limit(blocks, scratch=()):
    need = 2 * sum(_nbytes(s, d) for s, d in blocks) + sum(_nbytes(s, d) for s, d in scratch)
    return min(need + VMEM_TEMP_ALLOWANCE, V7X_VMEM_BYTES - 4 * 1024 * 1024)


def _rope_table_kernel(pos_ref, inv_ref, c_ref, s1_ref, s2_ref, *, half, keep_rest):
    ang = pos_ref[...] * inv_ref[...]
    lane = lax.broadcasted_iota(jnp.int32, ang.shape, 1)
    c = jnp.cos(ang)
    s = jnp.sin(ang)
    rest = 1.0 if keep_rest else 0.0
    c_ref[...] = jnp.where(lane < 2 * half, c, rest)
    s1_ref[...] = jnp.where((lane >= half) & (lane < 2 * half), s, 0.0)
    s2_ref[...] = jnp.where(lane < half, -s, 0.0)


def _rope_tables(pos_f32, rot_dim, keep_rest):
    t = pos_f32.shape[0]
    half = rot_dim // 2
    inv = ROPE_THETA ** (-jnp.arange(0, rot_dim, 2, dtype=F32) / rot_dim)
    inv_row = jnp.concatenate([inv, inv, jnp.zeros((LANES - rot_dim,), F32)])[None, :]
    tm = 1024
    out = jax.ShapeDtypeStruct((t, LANES), F32)
    spec = pl.BlockSpec((tm, LANES), lambda i: (i, 0))
    return pl.pallas_call(
        functools.partial(_rope_table_kernel, half=half, keep_rest=keep_rest),
        grid=(t // tm,),
        in_specs=[pl.BlockSpec((tm, 1), lambda i: (i, 0)),
                  pl.BlockSpec((1, LANES), lambda i: (0, 0))],
        out_specs=[spec, spec, spec],
        out_shape=[out, out, out],
        name="rope_tables",
    )(pos_f32, inv_row)


def _rope(y, c, s1, s2, half):
    return y * c + pltpu.roll(y, half, 1) * s1 + pltpu.roll(y, LANES - half, 1) * s2


def _rms(x, gain, n):
    ms = jnp.sum(x * x, axis=-1, keepdims=True) * (1.0 / n)
    return x * lax.rsqrt(ms + EPS) * gain


def _stream_stats_kernel(x_ref, xb_ref, ssq_ref):
    x = x_ref[...]
    xb_ref[...] = x.astype(BF16)
    sq = x * x
    part = sq[:, :LANES]
    for c in range(1, sq.shape[1] // LANES):
        part = part + sq[:, c * LANES:(c + 1) * LANES]
    ssq_ref[...] = part


def _stream_stats(x):
    t, d = x.shape
    tm = 256
    return pl.pallas_call(
        _stream_stats_kernel,
        grid=(t // tm,),
        in_specs=[pl.BlockSpec((tm, d), lambda i: (i, 0))],
        out_specs=[pl.BlockSpec((tm, d), lambda i: (i, 0)), pl.BlockSpec((tm, LANES), lambda i: (i, 0))],
        out_shape=[jax.ShapeDtypeStruct((t, d), BF16), jax.ShapeDtypeStruct((t, LANES), F32)],
        compiler_params=pltpu.CompilerParams(
            dimension_semantics=("arbitrary",),
            vmem_limit_bytes=_vmem_limit([((tm, d), F32), ((tm, d), BF16), ((tm, LANES), F32)])),
        name="stream_stats",
    )(x)


def _mm_body(*refs, nk, n_extra, n_out, epilogue, sub, has_norm, has_gain, stats):
    a_ref, w_ref = refs[0], refs[1]
    pos = 2
    ssq_ref = refs[pos] if has_norm else None
    pos += has_norm
    gain_ref = refs[pos] if has_gain else None
    pos += has_gain
    extra = refs[pos:pos + n_extra]
    outs = refs[pos + n_extra:pos + n_extra + n_out]
    tm = a_ref.shape[0]

    if stats:
        @pl.when((pl.program_id(1) == 0) & (pl.program_id(2) == 0))
        def _():
            outs[-1][...] = jnp.zeros_like(outs[-1])

    def weights():
        w = w_ref[...]
        if has_gain:
            g = gain_ref[...]
            w = jnp.concatenate([w[:, c * LANES:(c + 1) * LANES] * g for c in range(w.shape[1] // LANES)], axis=1)
        return w.astype(BF16)

    def scaled(acc, rows):
        if not has_norm:
            return acc
        ms = jnp.sum(ssq_ref[rows, :], axis=-1, keepdims=True) * (1.0 / D_MODEL)
        return acc * lax.rsqrt(ms + EPS)

    if nk == 1:
        w = weights()
        for r in range(tm // sub):
            rows = slice(r * sub, (r + 1) * sub)
            acc = jnp.dot(a_ref[rows, :], w, preferred_element_type=F32)
            epilogue(scaled(acc, rows), extra, outs, rows)
        return
    assert not has_norm
    acc_ref = refs[-1]
    k = pl.program_id(2)

    def partial_dots():
        w = weights()
        for r in range(tm // sub):
            rows = slice(r * sub, (r + 1) * sub)
            yield rows, jnp.dot(a_ref[rows, :], w, preferred_element_type=F32)

    @pl.when(k == 0)
    def _():
        for rows, d in partial_dots():
            acc_ref[rows, :] = d

    @pl.when((k > 0) & (k < nk - 1))
    def _():
        for rows, d in partial_dots():
            acc_ref[rows, :] += d

    @pl.when(k == nk - 1)
    def _():
        for rows, d in partial_dots():
            epilogue(acc_ref[rows, :] + d, extra, outs, rows)


def _matmul(a, w, *, tm, tn, tk, epilogue, extras=(), outs, name, sub=SUB_ROWS, layer=None, cols=None,
            norm=None, gain=None, stats=False):
    m, kdim = a.shape
    col0, n = cols if cols is not None else (0, w.shape[-1])
    nk = kdim // tk
    assert m % tm == 0 and n % tn == 0 and col0 % tn == 0 and kdim % tk == 0 and tm % min(sub, tm) == 0
    j0 = col0 // tn
    if layer is None:
        w_spec = pl.BlockSpec((tk, tn), lambda i, j, k: (k, j0 + j))
    else:
        w_spec = pl.BlockSpec((None, tk, tn), lambda i, j, k: (layer, k, j0 + j))
    in_specs = [pl.BlockSpec((tm, tk), lambda i, j, k: (i, k)), w_spec]
    blocks = [((tm, tk), a.dtype), ((tk, tn), w.dtype)]
    operands = [a, w]
    if norm is not None:
        extras = [(norm, (tm, LANES), lambda i, j, k: (i, 0))] + list(extras)
    if gain is not None:
        pos = 1 if norm is not None else 0
        extras = list(extras[:pos]) + [(gain, (tk, LANES), lambda i, j, k: (k, 0))] + list(extras[pos:])
    for arr, bshape, imap in extras:
        in_specs.append(pl.BlockSpec(bshape, imap))
        blocks.append((bshape, arr.dtype))
        operands.append(arr)
    out_specs, out_shape = [], []
    for dtype, ocols, bcols in outs:
        out_specs.append(pl.BlockSpec((tm, bcols), lambda i, j, k: (i, j)))
        out_shape.append(jax.ShapeDtypeStruct((m, ocols), dtype))
        blocks.append(((tm, bcols), dtype))
    if stats:
        out_specs.append(pl.BlockSpec((tm, tn), lambda i, j, k: (i, j)))
        out_shape.append(jax.ShapeDtypeStruct((m, n), BF16))
        out_specs.append(pl.BlockSpec((tm, LANES), lambda i, j, k: (i, 0)))
        out_shape.append(jax.ShapeDtypeStruct((m, LANES), F32))
        blocks += [((tm, tn), BF16), ((tm, LANES), F32)]
    scratch = [((tm, tn), F32)] if nk > 1 else []
    n_body = (norm is not None) + (gain is not None)
    return pl.pallas_call(
        functools.partial(_mm_body, nk=nk, n_extra=len(extras) - n_body, n_out=len(out_specs),
                          epilogue=epilogue, sub=min(sub, tm), has_norm=norm is not None,
                          has_gain=gain is not None, stats=stats),
        grid=(m // tm, n // tn, nk),
        in_specs=in_specs,
        out_specs=out_specs,
        out_shape=out_shape,
        scratch_shapes=[pltpu.VMEM(s, d) for s, d in scratch],
        compiler_params=pltpu.CompilerParams(
            dimension_semantics=("arbitrary", "arbitrary", "arbitrary"),
            vmem_limit_bytes=_vmem_limit(blocks, scratch)),
        name=name,
    )(*operands)


def _emit_stream(x_new, outs, rows):
    outs[0][rows, :] = x_new
    if len(outs) == 3:
        outs[1][rows, :] = x_new.astype(BF16)
        sq = x_new * x_new
        part = sq[:, :LANES]
        for c in range(1, sq.shape[1] // LANES):
            part = part + sq[:, c * LANES:(c + 1) * LANES]
        outs[2][rows, :] += part


def _ep_store(acc, extra, outs, rows):
    outs[0][rows, :] = acc.astype(outs[0].dtype)


def _ep_resid(acc, extra, outs, rows):
    _emit_stream(extra[0][rows, :] + acc, outs, rows)


def _ep_relu2(acc, extra, outs, rows):
    r = jnp.maximum(acc, 0.0)
    outs[0][rows, :] = (r * r).astype(outs[0].dtype)


def _ep_gate(acc, extra, outs, rows):
    res_ref, p_ref, wple_ref = extra
    ple = jnp.dot(p_ref[rows, :], wple_ref[...], preferred_element_type=F32)
    gate = jax.nn.sigmoid(acc)
    _emit_stream(res_ref[rows, :] + gate * ple, outs, rows)


def _ep_diff_qk(acc, extra, outs, rows):
    gain_ref, c_ref, s1_ref, s2_ref = extra
    c, s1, s2 = c_ref[rows, :], s1_ref[rows, :], s2_ref[rows, :]
    for g in range(acc.shape[1] // LANES):
        sl = slice(g * LANES, (g + 1) * LANES)
        y = _rms(acc[:, sl], gain_ref[:, sl], DIFF_HEAD_DIM)
        outs[0][rows, sl] = _rope(y, c, s1, s2, DIFF_ROT // 2).astype(outs[0].dtype)


def _ep_mla_in(acc, extra, outs, rows):
    gcq_ref, gckv_ref, gkr_ref, c_ref, s1_ref, s2_ref = extra
    cq_ref, ckv_ref, kr_ref = outs
    cq_ref[rows, :] = _rms(acc[:, :MLA_Q_RANK], gcq_ref[...], MLA_Q_RANK).astype(cq_ref.dtype)
    lo = MLA_Q_RANK
    ckv_ref[rows, :] = _rms(acc[:, lo:lo + MLA_KV_RANK], gckv_ref[...], MLA_KV_RANK).astype(ckv_ref.dtype)
    lo += MLA_KV_RANK
    y = _rms(acc[:, lo:lo + LANES], gkr_ref[...], MLA_ROPE)
    kr_ref[rows, :] = _rope(y, c_ref[rows, :], s1_ref[rows, :], s2_ref[rows, :],
                            MLA_ROPE // 2).astype(kr_ref.dtype)


def _ep_mla_q(acc, extra, outs, rows):
    gn_ref, gr_ref, c_ref, s1_ref, s2_ref = extra
    c, s1, s2 = c_ref[rows, :], s1_ref[rows, :], s2_ref[rows, :]
    for h in range(acc.shape[1] // MLA_QK_PAD):
        lo = h * MLA_QK_PAD
        yn = _rms(acc[:, lo:lo + LANES], gn_ref[...], MLA_NOPE)
        outs[0][rows, lo:lo + LANES] = yn.astype(outs[0].dtype)
        yr = _rms(acc[:, lo + LANES:lo + 2 * LANES], gr_ref[...], MLA_ROPE)
        outs[0][rows, lo + LANES:lo + 2 * LANES] = _rope(yr, c, s1, s2, MLA_ROPE // 2).astype(outs[0].dtype)


def _ep_mla_kv(acc, extra, outs, rows):
    gk_ref, kr_ref, a_ref, wuv_ref = extra
    k_ref, v_ref = outs
    kr = kr_ref[rows, :]
    v = jnp.dot(a_ref[rows, :], wuv_ref[...], preferred_element_type=F32).astype(v_ref.dtype)
    ones = jnp.ones((acc.shape[0], LANES), v_ref.dtype)
    for h in range(acc.shape[1] // LANES):
        yk = _rms(acc[:, h * LANES:(h + 1) * LANES], gk_ref[...], MLA_NOPE)
        lo = h * MLA_QK_PAD
        k_ref[rows, lo:lo + LANES] = yk.astype(k_ref.dtype)
        k_ref[rows, lo + LANES:lo + 2 * LANES] = kr
        lo = h * MLA_V_PAD
        v_ref[rows, lo:lo + LANES] = v[:, h * LANES:(h + 1) * LANES]
        v_ref[rows, lo + LANES:lo + 2 * LANES] = ones


SUBLANE_PARTIALS = 32


def _fold_rows(x, op):
    r = x.shape[0] // SUBLANE_PARTIALS
    return op(x.reshape(r, SUBLANE_PARTIALS, x.shape[1]), axis=0)


def _score_sweep_t(q, k_ref, k_cols, s_ref, tk):
    mrun = None
    for j in range(s_ref.shape[0]):
        s = lax.dot_general(k_ref[j * tk:(j + 1) * tk, k_cols], q, (((1,), (1,)), ((), ())),
                            preferred_element_type=F32)
        s_ref[j] = s
        part = _fold_rows(s, jnp.max)
        mrun = part if mrun is None else jnp.maximum(mrun, part)
    return jnp.max(mrun, axis=0, keepdims=True)


def _diff_scores(q_ref, rows, k_ref, bufs, tk):
    ms = []
    for c, buf in enumerate(bufs):
        cols = slice(c * DIFF_HEAD_DIM, (c + 1) * DIFF_HEAD_DIM)
        ms.append(_score_sweep_t(q_ref[rows, cols], k_ref, cols, buf, tk))
    return ms


def _diff_stage(q_ref, rows_next, k_ref, bufs_next, bufs, ms, vt_ref, lam_full, gsub_ref, scale_out, tk):
    nchunk = bufs[0].shape[0]
    q_next = [q_ref[rows_next, c * DIFF_HEAD_DIM:(c + 1) * DIFF_HEAD_DIM] for c in range(2)]
    mrun, lrun = [None, None], [None, None]
    for j in range(nchunk):
        keys = slice(j * tk, (j + 1) * tk)
        for c in range(2):
            cols = slice(c * DIFF_HEAD_DIM, (c + 1) * DIFF_HEAD_DIM)
            s = lax.dot_general(k_ref[keys, cols], q_next[c], (((1,), (1,)), ((), ())),
                                preferred_element_type=F32)
            bufs_next[c][j] = s
            part = _fold_rows(s, jnp.max)
            mrun[c] = part if mrun[c] is None else jnp.maximum(mrun[c], part)
        for c in range(2):
            e = jnp.exp2(bufs[c][j] - ms[c])
            bufs[c][j] = e
            part = _fold_rows(e, jnp.sum)
            lrun[c] = part if lrun[c] is None else lrun[c] + part
    l0 = jnp.sum(lrun[0], axis=0, keepdims=True)
    l1 = jnp.sum(lrun[1], axis=0, keepdims=True)
    rho = lam_full * l0 / l1
    acc = None
    for j in range(nchunk):
        a = bufs[0][j] - bufs[1][j] * rho
        d = jnp.dot(vt_ref[:, j * tk:(j + 1) * tk], a.astype(BF16), preferred_element_type=F32)
        acc = d if acc is None else acc + d
    acc = acc * (1.0 / l0)
    ms_o = jnp.sum(acc * acc, axis=0, keepdims=True) * (1.0 / (2 * DIFF_HEAD_DIM))
    y = acc * lax.rsqrt(ms_o + EPS) * gsub_ref[...] * scale_out
    m_next = [jnp.max(m, axis=0, keepdims=True) for m in mrun]
    return m_next, y.T


def _diff_attn_kernel(lam_ref, gsub_ref, q_ref, k_ref, v_ref, o_ref, a0_ref, a1_ref, b0_ref, b1_ref,
                      m_ref, vt_ref, *, tq, tk, lambda_init):
    lam = lam_ref[...]
    lam_full = (jnp.exp(jnp.sum(lam[0:1] * lam[1:2], axis=-1, keepdims=True))
                - jnp.exp(jnp.sum(lam[2:3] * lam[3:4], axis=-1, keepdims=True)) + lambda_init)
    vt_ref[...] = v_ref[...].T
    npair = q_ref.shape[0] // (2 * tq)
    bufs_a, bufs_b = (a0_ref, a1_ref), (b0_ref, b1_ref)
    m0, m1 = _diff_scores(q_ref, slice(0, tq), k_ref, bufs_a, tk)
    m_ref[0:1, :] = m0
    m_ref[1:2, :] = m1

    def pair(i, carry):
        even = pl.ds(pl.multiple_of(2 * i * tq, tq), tq)
        odd = pl.ds(pl.multiple_of((2 * i + 1) * tq, tq), tq)
        ms_b, y = _diff_stage(q_ref, odd, k_ref, bufs_b, bufs_a, [m_ref[0:1, :], m_ref[1:2, :]], vt_ref,
                              lam_full, gsub_ref, 1.0 - lambda_init, tk)
        o_ref[even, :] = y.astype(o_ref.dtype)
        nxt = pl.ds(pl.multiple_of(2 * jnp.minimum(i + 1, npair - 1) * tq, tq), tq)
        ms_a, y = _diff_stage(q_ref, nxt, k_ref, bufs_a, bufs_b, ms_b, vt_ref,
                              lam_full, gsub_ref, 1.0 - lambda_init, tk)
        o_ref[odd, :] = y.astype(o_ref.dtype)
        m_ref[0:1, :] = ms_a[0]
        m_ref[1:2, :] = ms_a[1]
        return carry

    lax.fori_loop(0, npair, pair, 0)


def _diff_attention(qk, v, lam, g_sub, lambda_init, batch, seq):
    tq, tk = DIFF_TQ, ATT_TK
    hw = 2 * DIFF_HEAD_DIM
    blocks = [((seq, hw), BF16)] * 4 + [((hw, tq), F32)]
    scratch = [((seq // tk, tk, tq), F32)] * 4 + [((2, tq), F32), ((hw, seq), BF16)]
    return pl.pallas_call(
        functools.partial(_diff_attn_kernel, tq=tq, tk=tk, lambda_init=lambda_init),
        grid=(batch, DIFF_HEADS),
        in_specs=[pl.BlockSpec((4, DIFF_HEAD_DIM), lambda b, h: (0, 0)),
                  pl.BlockSpec((hw, tq), lambda b, h: (0, 0)),
                  pl.BlockSpec((seq, hw), lambda b, h: (b, h)),
                  pl.BlockSpec((seq, hw), lambda b, h: (b, DIFF_HEADS + h)),
                  pl.BlockSpec((seq, hw), lambda b, h: (b, h))],
        out_specs=pl.BlockSpec((seq, hw), lambda b, h: (b, h)),
        out_shape=jax.ShapeDtypeStruct((batch * seq, DIFF_V_WIDTH), BF16),
        scratch_shapes=[pltpu.VMEM(s, d) for s, d in scratch],
        compiler_params=pltpu.CompilerParams(
            dimension_semantics=("arbitrary", "arbitrary"),
            vmem_limit_bytes=_vmem_limit(blocks, scratch)),
        name="diff_attention",
    )(lam, jnp.broadcast_to(g_sub[:, None], (hw, tq)), qk, qk, v)


def _score_sweep(q, k_ref, s_ref, tk):
    mpart = None
    for j in range(s_ref.shape[0]):
        s = lax.dot_general(q, k_ref[j * tk:(j + 1) * tk, :], (((1,), (1,)), ((), ())),
                            preferred_element_type=F32)
        s_ref[j] = s
        for g in range(tk // LANES):
            sg = s[:, g * LANES:(g + 1) * LANES]
            mpart = sg if mpart is None else jnp.maximum(mpart, sg)
    return jnp.max(mpart, axis=-1, keepdims=True)


def _fused_sweep(q_next, k_ref, s_next_ref, s_ref, m, v_ref, tk):
    mpart, acc = None, None
    for j in range(s_ref.shape[0]):
        s = lax.dot_general(q_next, k_ref[j * tk:(j + 1) * tk, :], (((1,), (1,)), ((), ())),
                            preferred_element_type=F32)
        s_next_ref[j] = s
        e = jnp.exp2(s_ref[j] - m)
        for g in range(tk // LANES):
            sg = s[:, g * LANES:(g + 1) * LANES]
            mpart = sg if mpart is None else jnp.maximum(mpart, sg)
        d = jnp.dot(e.astype(BF16), v_ref[j * tk:(j + 1) * tk, :], preferred_element_type=F32)
        acc = d if acc is None else acc + d
    return jnp.max(mpart, axis=-1, keepdims=True), acc


def _mla_attn_kernel(q_ref, k_ref, v_ref, o_ref, s0_ref, s1_ref, m0_ref, *, tq, tk):
    npair = q_ref.shape[0] // (2 * tq)
    m0_ref[...] = _score_sweep(q_ref[0:tq, :], k_ref, s0_ref, tk)

    def finish(acc, rows):
        o_ref[rows, :] = (acc[:, :MLA_V] / acc[:, MLA_V:MLA_V + 1]).astype(o_ref.dtype)

    def pair(i, carry):
        even = pl.ds(pl.multiple_of(2 * i * tq, tq), tq)
        odd = pl.ds(pl.multiple_of((2 * i + 1) * tq, tq), tq)
        m1, acc = _fused_sweep(q_ref[odd, :], k_ref, s1_ref, s0_ref, m0_ref[...], v_ref, tk)
        finish(acc, even)
        nxt = pl.ds(pl.multiple_of(2 * jnp.minimum(i + 1, npair - 1) * tq, tq), tq)
        m0, acc = _fused_sweep(q_ref[nxt, :], k_ref, s0_ref, s1_ref, m1, v_ref, tk)
        m0_ref[...] = m0
        finish(acc, odd)
        return carry

    lax.fori_loop(0, npair, pair, 0)


def _mla_attention(q, k, v, batch, seq):
    tq, tk = ATT_TQ, MLA_TK
    blocks = [((seq, MLA_QK_PAD), BF16)] * 2 + [((seq, MLA_V_PAD), BF16), ((seq, MLA_V), BF16)]
    scratch = [((seq // tk, tq, tk), F32)] * 2 + [((tq, 1), F32)]
    return pl.pallas_call(
        functools.partial(_mla_attn_kernel, tq=tq, tk=tk),
        grid=(batch, MLA_HEADS),
        in_specs=[pl.BlockSpec((seq, MLA_QK_PAD), lambda b, h: (b, h)),
                  pl.BlockSpec((seq, MLA_QK_PAD), lambda b, h: (b, h)),
                  pl.BlockSpec((seq, MLA_V_PAD), lambda b, h: (b, h))],
        out_specs=pl.BlockSpec((seq, MLA_V), lambda b, h: (b, h)),
        out_shape=jax.ShapeDtypeStruct((batch * seq, MLA_HEADS * MLA_V), BF16),
        scratch_shapes=[pltpu.VMEM(s, d) for s, d in scratch],
        compiler_params=pltpu.CompilerParams(
            dimension_semantics=("arbitrary", "arbitrary"),
            vmem_limit_bytes=_vmem_limit(blocks, scratch)),
        name="mla_attention",
    )(q, k, v)


def _row_spec(cols):
    return (TM, cols), (lambda i, j, k: (i, 0))


def _lane_gain(g):
    return jnp.broadcast_to(g[:, None], (g.shape[0], LANES))


def _diff_mixer(stream, g_mix, w_in, w_out, layer, g_q, g_k, lam, g_sub, tabs, lambda_init, batch, seq):
    x, xb, ssq = stream
    qscale = DIFF_HEAD_DIM ** -0.5 * LOG2E
    gain = jnp.concatenate([jnp.tile(g_q * qscale, 2 * DIFF_HEADS), jnp.tile(g_k, 2 * DIFF_HEADS)])[None, :]
    tab_extras = [(t, *_row_spec(LANES)) for t in tabs]
    g_rows = _lane_gain(g_mix)
    (qk,) = _matmul(xb, w_in, layer=layer, cols=(0, 2 * DIFF_QK_WIDTH), tm=TM, tn=TN, tk=D_MODEL,
                    norm=ssq, gain=g_rows, epilogue=_ep_diff_qk,
                    extras=[(gain, (1, TN), lambda i, j, k: (0, j))] + tab_extras,
                    outs=[(BF16, 2 * DIFF_QK_WIDTH, TN)], name="diff_qk_proj")
    (v,) = _matmul(xb, w_in, layer=layer, cols=(2 * DIFF_QK_WIDTH, DIFF_V_WIDTH), tm=TM, tn=TN, tk=D_MODEL,
                   norm=ssq, gain=g_rows, epilogue=_ep_store, outs=[(BF16, DIFF_V_WIDTH, TN)],
                   name="diff_v_proj")
    o = _diff_attention(qk, v, lam, g_sub, lambda_init, batch, seq)
    return _matmul(o, w_out, layer=layer, tm=TM, tn=TN, tk=DIFF_V_WIDTH, epilogue=_ep_resid, stats=True,
                   extras=[(x, (TM, TN), lambda i, j, k: (i, j))],
                   outs=[(F32, D_MODEL, TN)], name="diff_out_proj")


def _mla_mixer(stream, g_mix, w_in, g_cq, g_ckv, w_uq, w_ukv, g_q, g_k, w_out, layer, tabs, batch, seq):
    x, xb, ssq = stream
    tab_extras = [(t, *_row_spec(LANES)) for t in tabs]
    zpad = jnp.zeros((MLA_ROPE,), F32)
    n_in = MLA_Q_RANK + MLA_KV_RANK + LANES
    w_in_p = jnp.pad(w_in * g_mix[:, None], ((0, 0), (0, n_in - w_in.shape[1]))).astype(BF16)
    const = lambda i, j, k: (0, 0)
    tm_in = 512
    cq, ckv, kr = _matmul(
        xb, w_in_p, tm=tm_in, tn=n_in, tk=D_MODEL, norm=ssq, epilogue=_ep_mla_in,
        extras=[(g_cq[None, :], (1, MLA_Q_RANK), const), (g_ckv[None, :], (1, MLA_KV_RANK), const),
                (jnp.concatenate([g_k[MLA_NOPE:], zpad])[None, :], (1, LANES), const)]
               + [(t, (tm_in, LANES), lambda i, j, k: (i, 0)) for t in tabs],
        outs=[(BF16, MLA_Q_RANK, MLA_Q_RANK), (BF16, MLA_KV_RANK, MLA_KV_RANK), (BF16, LANES, LANES)],
        name="mla_in_proj")
    w_uq_h = w_uq.reshape(MLA_Q_RANK, MLA_HEADS, MLA_QK)
    w_uq_p = jnp.pad(w_uq_h, ((0, 0), (0, 0), (0, MLA_QK_PAD - MLA_QK))).reshape(
        MLA_Q_RANK, MLA_HEADS * MLA_QK_PAD).astype(BF16)
    qscale = MLA_QK ** -0.5 * LOG2E
    g_qs = g_q * qscale
    (q,) = _matmul(
        cq, w_uq_p, tm=TM, tn=TN, tk=MLA_Q_RANK, epilogue=_ep_mla_q,
        extras=[(g_qs[None, :MLA_NOPE], (1, LANES), const),
                (jnp.concatenate([g_qs[MLA_NOPE:], zpad])[None, :], (1, LANES), const)] + tab_extras,
        outs=[(BF16, MLA_HEADS * MLA_QK_PAD, TN)], name="mla_q_proj")
    w_ukv_h = w_ukv.reshape(MLA_KV_RANK, MLA_HEADS, MLA_NOPE + MLA_V).astype(BF16)
    w_uk = w_ukv_h[:, :, :MLA_NOPE].reshape(MLA_KV_RANK, MLA_HEADS * MLA_NOPE)
    w_uv = w_ukv_h[:, :, MLA_NOPE:].reshape(MLA_KV_RANK, MLA_HEADS * MLA_V)
    k, v = _matmul(
        ckv, w_uk, tm=TM, tn=TN, tk=MLA_KV_RANK, epilogue=_ep_mla_kv,
        extras=[(g_k[None, :MLA_NOPE], (1, LANES), const),
                (kr, (TM, LANES), lambda i, j, k: (i, 0)),
                (ckv, (TM, MLA_KV_RANK), lambda i, j, k: (i, 0)),
                (w_uv, (MLA_KV_RANK, TN), lambda i, j, k: (0, j))],
        outs=[(BF16, MLA_HEADS * MLA_QK_PAD, 2 * TN), (BF16, MLA_HEADS * MLA_V_PAD, 2 * TN)],
        name="mla_kv_proj")
    o = _mla_attention(q, k, v, batch, seq)
    return _matmul(o, w_out, layer=layer, tm=TM, tn=TN, tk=MLA_HEADS * MLA_V, epilogue=_ep_resid, stats=True,
                   extras=[(x, (TM, TN), lambda i, j, k: (i, j))],
                   outs=[(F32, D_MODEL, TN)], name="mla_out_proj")


def kernel(x, p, positions, g_mix, g_mlp, g_ple, w1, w2, w_gate, w_ple, diff_w_in, diff_w_out, diff_g_q, diff_g_k, diff_lambda, diff_g_sub, mla_w_in, mla_g_cq, mla_g_ckv, mla_w_uq, mla_w_ukv, mla_g_q, mla_g_k, mla_w_out):
    batch, seq, d = x.shape
    t = batch * seq
    x = x.reshape(t, d)
    pos = positions.reshape(t, 1).astype(F32)
    tabs_d = _rope_tables(pos, DIFF_ROT, keep_rest=True)
    tabs_m = _rope_tables(pos, MLA_ROPE, keep_rest=False)
    p_bf = p.reshape(DEPTH, t, PLE_DIM).astype(BF16)
    stream = (x, *_stream_stats(x))
    for i in range(DEPTH):
        j = i // N_MIXERS
        if i % N_MIXERS == 0:
            lambda_init = 0.8 - 0.6 * math.exp(-0.3 * i)
            stream = _diff_mixer(stream, g_mix[i], diff_w_in, diff_w_out, j, diff_g_q[j], diff_g_k[j],
                                 diff_lambda[j], diff_g_sub[j], tabs_d, lambda_init, batch, seq)
        else:
            stream = _mla_mixer(stream, g_mix[i], mla_w_in[j], mla_g_cq[j], mla_g_ckv[j], mla_w_uq[j],
                                mla_w_ukv[j], mla_g_q[j], mla_g_k[j], mla_w_out, j, tabs_m, batch, seq)
        x, xb, ssq = stream
        (a,) = _matmul(xb, w1, layer=i, tm=TM, tn=TN, tk=D_MODEL, norm=ssq, gain=_lane_gain(g_mlp[i]),
                       epilogue=_ep_relu2, outs=[(BF16, D_FF, TN)], name="mlp_up")
        x, xb, ssq = _matmul(a, w2, layer=i, tm=TM, tn=2 * TN, tk=2048, epilogue=_ep_resid, stats=True,
                             extras=[(x, (TM, 2 * TN), lambda i_, j_, k_: (i_, j_))],
                             outs=[(F32, D_MODEL, 2 * TN)], name="mlp_down")
        stream = _matmul(xb, w_gate, layer=i, tm=TM, tn=TN, tk=D_MODEL, norm=ssq, gain=_lane_gain(g_ple[i]),
                         epilogue=_ep_gate, stats=i + 1 < DEPTH,
                         extras=[(x, (TM, TN), lambda i_, j_, k_: (i_, j_)),
                                 (p_bf[i], (TM, PLE_DIM), lambda i_, j_, k_: (i_, 0)),
                                 (w_ple[i].astype(BF16), (PLE_DIM, TN), lambda i_, j_, k_: (0, j_))],
                         outs=[(F32, D_MODEL, TN)], name="ple_gate")
    return stream[0].reshape(batch, seq, d)
```

```python
import functools
import math

import jax
import jax.numpy as jnp
from jax import lax
from jax.experimental import pallas as pl
from jax.experimental.pallas import tpu as pltpu

D_MODEL = 4096
DEPTH = 4
N_MIXERS = 2
ROPE_THETA = 500000.0
EPS = 1e-6
D_FF = 4 * D_MODEL
PLE_DIM = 256

DIFF_HEAD_DIM = 128
DIFF_HEADS = D_MODEL // (2 * DIFF_HEAD_DIM)
DIFF_ROT = DIFF_HEAD_DIM // 4
DIFF_QK_WIDTH = DIFF_HEADS * 2 * DIFF_HEAD_DIM
DIFF_V_WIDTH = DIFF_HEADS * 2 * DIFF_HEAD_DIM

MLA_HEADS = 32
MLA_NOPE = 128
MLA_ROPE = 64
MLA_V = 128
MLA_QK = MLA_NOPE + MLA_ROPE
MLA_Q_RANK = 1024
MLA_KV_RANK = 512

LANES = 128
MLA_QK_PAD = 2 * LANES
MLA_V_PAD = 2 * LANES
V7X_VMEM_BYTES = 64 * 1024 * 1024
VMEM_TEMP_ALLOWANCE = 12 * 1024 * 1024
LOG2E = math.log2(math.e)

TM = 1024
TN = 512
SUB_ROWS = 256
ATT_TQ = 512
DIFF_TQ = 256
ATT_TK = 1024
MLA_TK = 512

F32 = jnp.float32
BF16 = jnp.bfloat16


def _nbytes(shape, dtype):
    return math.prod(shape) * jnp.dtype(dtype).itemsize


def _vmem_limit(blocks, scratch=()):
    need = 2 * sum(_nbytes(s, d) for s, d in blocks) + sum(_nbytes(s, d) for s, d in scratch)
    return min(need + VMEM_TEMP_ALLOWANCE, V7X_VMEM_BYTES - 4 * 1024 * 1024)


def _rope_table_kernel(pos_ref, inv_ref, c_ref, s1_ref, s2_ref, *, half, keep_rest):
    ang = pos_ref[...] * inv_ref[...]
    lane = lax.broadcasted_iota(jnp.int32, ang.shape, 1)
    c = jnp.cos(ang)
    s = jnp.sin(ang)
    rest = 1.0 if keep_rest else 0.0
    c_ref[...] = jnp.where(lane < 2 * half, c, rest)
    s1_ref[...] = jnp.where((lane >= half) & (lane < 2 * half), s, 0.0)
    s2_ref[...] = jnp.where(lane < half, -s, 0.0)


def _rope_tables(pos_f32, rot_dim, keep_rest):
    t = pos_f32.shape[0]
    half = rot_dim // 2
    inv = ROPE_THETA ** (-jnp.arange(0, rot_dim, 2, dtype=F32) / rot_dim)
    inv_row = jnp.concatenate([inv, inv, jnp.zeros((LANES - rot_dim,), F32)])[None, :]
    tm = 1024
    out = jax.ShapeDtypeStruct((t, LANES), F32)
    spec = pl.BlockSpec((tm, LANES), lambda i: (i, 0))
    return pl.pallas_call(
        functools.partial(_rope_table_kernel, half=half, keep_rest=keep_rest),
        grid=(t // tm,),
        in_specs=[pl.BlockSpec((tm, 1), lambda i: (i, 0)),
                  pl.BlockSpec((1, LANES), lambda i: (0, 0))],
        out_specs=[spec, spec, spec],
        out_shape=[out, out, out],
        name="rope_tables",
    )(pos_f32, inv_row)


def _rope(y, c, s1, s2, half):
    return y * c + pltpu.roll(y, half, 1) * s1 + pltpu.roll(y, LANES - half, 1) * s2


def _rms(x, gain, n):
    ms = jnp.sum(x * x, axis=-1, keepdims=True) * (1.0 / n)
    return x * lax.rsqrt(ms + EPS) * gain


def _stream_stats_kernel(x_ref, xb_ref, ssq_ref):
    x = x_ref[...]
    xb_ref[...] = x.astype(BF16)
    sq = x * x
    part = sq[:, :LANES]
    for c in range(1, sq.shape[1] // LANES):
        part = part + sq[:, c * LANES:(c + 1) * LANES]
    ssq_ref[...] = part


def _stream_stats(x):
    t, d = x.shape
    tm = 256
    return pl.pallas_call(
        _stream_stats_kernel,
        grid=(t // tm,),
        in_specs=[pl.BlockSpec((tm, d), lambda i: (i, 0))],
        out_specs=[pl.BlockSpec((tm, d), lambda i: (i, 0)), pl.BlockSpec((tm, LANES), lambda i: (i, 0))],
        out_shape=[jax.ShapeDtypeStruct((t, d), BF16), jax.ShapeDtypeStruct((t, LANES), F32)],
        compiler_params=pltpu.CompilerParams(
            dimension_semantics=("arbitrary",),
            vmem_limit_bytes=_vmem_limit([((tm, d), F32), ((tm, d), BF16), ((tm, LANES), F32)])),
        name="stream_stats",
    )(x)


def _mm_body(*refs, nk, n_extra, n_out, epilogue, sub, has_norm, has_gain, stats):
    a_ref, w_ref = refs[0], refs[1]
    pos = 2
    ssq_ref = refs[pos] if has_norm else None
    pos += has_norm
    gain_ref = refs[pos] if has_gain else None
    pos += has_gain
    extra = refs[pos:pos + n_extra]
    outs = refs[pos + n_extra:pos + n_extra + n_out]
    tm = a_ref.shape[0]

    if stats:
        @pl.when((pl.program_id(1) == 0) & (pl.program_id(2) == 0))
        def _():
            outs[-1][...] = jnp.zeros_like(outs[-1])

    def weights():
        w = w_ref[...]
        if has_gain:
            g = gain_ref[...]
            w = jnp.concatenate([w[:, c * LANES:(c + 1) * LANES] * g for c in range(w.shape[1] // LANES)], axis=1)
        return w.astype(BF16)

    def scaled(acc, rows):
        if not has_norm:
            return acc
        ms = jnp.sum(ssq_ref[rows, :], axis=-1, keepdims=True) * (1.0 / D_MODEL)
        return acc * lax.rsqrt(ms + EPS)

    if nk == 1:
        w = weights()
        for r in range(tm // sub):
            rows = slice(r * sub, (r + 1) * sub)
            acc = jnp.dot(a_ref[rows, :], w, preferred_element_type=F32)
            epilogue(scaled(acc, rows), extra, outs, rows)
        return
    assert not has_norm
    acc_ref = refs[-1]
    k = pl.program_id(2)

    def partial_dots():
        w = weights()
        for r in range(tm // sub):
            rows = slice(r * sub, (r + 1) * sub)
            yield rows, jnp.dot(a_ref[rows, :], w, preferred_element_type=F32)

    @pl.when(k == 0)
    def _():
        for rows, d in partial_dots():
            acc_ref[rows, :] = d

    @pl.when((k > 0) & (k < nk - 1))
    def _():
        for rows, d in partial_dots():
            acc_ref[rows, :] += d

    @pl.when(k == nk - 1)
    def _():
        for rows, d in partial_dots():
            epilogue(acc_ref[rows, :] + d, extra, outs, rows)


def _matmul(a, w, *, tm, tn, tk, epilogue, extras=(), outs, name, sub=SUB_ROWS, layer=None, cols=None,
            norm=None, gain=None, stats=False):
    m, kdim = a.shape
    col0, n = cols if cols is not None else (0, w.shape[-1])
    nk = kdim // tk
    assert m % tm == 0 and n % tn == 0 and col0 % tn == 0 and kdim % tk == 0 and tm % min(sub, tm) == 0
    j0 = col0 // tn
    if layer is None:
        w_spec = pl.BlockSpec((tk, tn), lambda i, j, k: (k, j0 + j))
    else:
        w_spec = pl.BlockSpec((None, tk, tn), lambda i, j, k: (layer, k, j0 + j))
    in_specs = [pl.BlockSpec((tm, tk), lambda i, j, k: (i, k)), w_spec]
    blocks = [((tm, tk), a.dtype), ((tk, tn), w.dtype)]
    operands = [a, w]
    if norm is not None:
        extras = [(norm, (tm, LANES), lambda i, j, k: (i, 0))] + list(extras)
    if gain is not None:
        pos = 1 if norm is not None else 0
        extras = list(extras[:pos]) + [(gain, (tk, LANES), lambda i, j, k: (k, 0))] + list(extras[pos:])
    for arr, bshape, imap in extras:
        in_specs.append(pl.BlockSpec(bshape, imap))
        blocks.append((bshape, arr.dtype))
        operands.append(arr)
    out_specs, out_shape = [], []
    for dtype, ocols, bcols in outs:
        out_specs.append(pl.BlockSpec((tm, bcols), lambda i, j, k: (i, j)))
        out_shape.append(jax.ShapeDtypeStruct((m, ocols), dtype))
        blocks.append(((tm, bcols), dtype))
    if stats:
        out_specs.append(pl.BlockSpec((tm, tn), lambda i, j, k: (i, j)))
        out_shape.append(jax.ShapeDtypeStruct((m, n), BF16))
        out_specs.append(pl.BlockSpec((tm, LANES), lambda i, j, k: (i, 0)))
        out_shape.append(jax.ShapeDtypeStruct((m, LANES), F32))
        blocks += [((tm, tn), BF16), ((tm, LANES), F32)]
    scratch = [((tm, tn), F32)] if nk > 1 else []
    n_body = (norm is not None) + (gain is not None)
    return pl.pallas_call(
        functools.partial(_mm_body, nk=nk, n_extra=len(extras) - n_body, n_out=len(out_specs),
                          epilogue=epilogue, sub=min(sub, tm), has_norm=norm is not None,
                          has_gain=gain is not None, stats=stats),
        grid=(m // tm, n // tn, nk),
        in_specs=in_specs,
        out_specs=out_specs,
        out_shape=out_shape,
        scratch_shapes=[pltpu.VMEM(s, d) for s, d in scratch],
        compiler_params=pltpu.CompilerParams(
            dimension_semantics=("arbitrary", "arbitrary", "arbitrary"),
            vmem_limit_bytes=_vmem_limit(blocks, scratch)),
        name=name,
    )(*operands)


def _emit_stream(x_new, outs, rows):
    outs[0][rows, :] = x_new
    if len(outs) == 3:
        outs[1][rows, :] = x_new.astype(BF16)
        sq = x_new * x_new
        part = sq[:, :LANES]
        for c in range(1, sq.shape[1] // LANES):
            part = part + sq[:, c * LANES:(c + 1) * LANES]
        outs[2][rows, :] += part


def _ep_store(acc, extra, outs, rows):
    outs[0][rows, :] = acc.astype(outs[0].dtype)


def _ep_resid(acc, extra, outs, rows):
    _emit_stream(extra[0][rows, :] + acc, outs, rows)


def _ep_relu2(acc, extra, outs, rows):
    r = jnp.maximum(acc, 0.0)
    outs[0][rows, :] = (r * r).astype(outs[0].dtype)


def _ep_gate(acc, extra, outs, rows):
    res_ref, p_ref, wple_ref = extra
    ple = jnp.dot(p_ref[rows, :], wple_ref[...], preferred_element_type=F32)
    gate = jax.nn.sigmoid(acc)
    _emit_stream(res_ref[rows, :] + gate * ple, outs, rows)


def _ep_diff_qk(acc, extra, outs, rows):
    gain_ref, c_ref, s1_ref, s2_ref = extra
    c, s1, s2 = c_ref[rows, :], s1_ref[rows, :], s2_ref[rows, :]
    for g in range(acc.shape[1] // LANES):
        sl = slice(g * LANES, (g + 1) * LANES)
        y = _rms(acc[:, sl], gain_ref[:, sl], DIFF_HEAD_DIM)
        outs[0][rows, sl] = _rope(y, c, s1, s2, DIFF_ROT // 2).astype(outs[0].dtype)


def _ep_mla_in(acc, extra, outs, rows):
    gcq_ref, gckv_ref, gkr_ref, c_ref, s1_ref, s2_ref = extra
    cq_ref, ckv_ref, kr_ref = outs
    cq_ref[rows, :] = _rms(acc[:, :MLA_Q_RANK], gcq_ref[...], MLA_Q_RANK).astype(cq_ref.dtype)
    lo = MLA_Q_RANK
    ckv_ref[rows, :] = _rms(acc[:, lo:lo + MLA_KV_RANK], gckv_ref[...], MLA_KV_RANK).astype(ckv_ref.dtype)
    lo += MLA_KV_RANK
    y = _rms(acc[:, lo:lo + LANES], gkr_ref[...], MLA_ROPE)
    kr_ref[rows, :] = _rope(y, c_ref[rows, :], s1_ref[rows, :], s2_ref[rows, :],
                            MLA_ROPE // 2).astype(kr_ref.dtype)


def _ep_mla_q(acc, extra, outs, rows):
    gn_ref, gr_ref, c_ref, s1_ref, s2_ref = extra
    c, s1, s2 = c_ref[rows, :], s1_ref[rows, :], s2_ref[rows, :]
    for h in range(acc.shape[1] // MLA_QK_PAD):
        lo = h * MLA_QK_PAD
        yn = _rms(acc[:, lo:lo + LANES], gn_ref[...], MLA_NOPE)
        outs[0][rows, lo:lo + LANES] = yn.astype(outs[0].dtype)
        yr = _rms(acc[:, lo + LANES:lo + 2 * LANES], gr_ref[...], MLA_ROPE)
        outs[0][rows, lo + LANES:lo + 2 * LANES] = _rope(yr, c, s1, s2, MLA_ROPE // 2).astype(outs[0].dtype)


def _ep_mla_kv(acc, extra, outs, rows):
    gk_ref, kr_ref, a_ref, wuv_ref = extra
    k_ref, v_ref = outs
    kr = kr_ref[rows, :]
    v = jnp.dot(a_ref[rows, :], wuv_ref[...], preferred_element_type=F32).astype(v_ref.dtype)
    ones = jnp.ones((acc.shape[0], LANES), v_ref.dtype)
    for h in range(acc.shape[1] // LANES):
        yk = _rms(acc[:, h * LANES:(h + 1) * LANES], gk_ref[...], MLA_NOPE)
        lo = h * MLA_QK_PAD
        k_ref[rows, lo:lo + LANES] = yk.astype(k_ref.dtype)
        k_ref[rows, lo + LANES:lo + 2 * LANES] = kr
        lo = h * MLA_V_PAD
        v_ref[rows, lo:lo + LANES] = v[:, h * LANES:(h + 1) * LANES]
        v_ref[rows, lo + LANES:lo + 2 * LANES] = ones


SUBLANE_PARTIALS = 32


def _fold_rows(x, op):
    r = x.shape[0] // SUBLANE_PARTIALS
    return op(x.reshape(r, SUBLANE_PARTIALS, x.shape[1]), axis=0)


def _score_sweep_t(q, k_ref, k_cols, s_ref, tk):
    mrun = None
    for j in range(s_ref.shape[0]):
        s = lax.dot_general(k_ref[j * tk:(j + 1) * tk, k_cols], q, (((1,), (1,)), ((), ())),
                            preferred_element_type=F32)
        s_ref[j] = s
        part = _fold_rows(s, jnp.max)
        mrun = part if mrun is None else jnp.maximum(mrun, part)
    return jnp.max(mrun, axis=0, keepdims=True)


def _diff_scores(q_ref, rows, k_ref, bufs, tk):
    ms = []
    for c, buf in enumerate(bufs):
        cols = slice(c * DIFF_HEAD_DIM, (c + 1) * DIFF_HEAD_DIM)
        ms.append(_score_sweep_t(q_ref[rows, cols], k_ref, cols, buf, tk))
    return ms


def _diff_numerators(bufs, ms):
    ls = []
    for buf, m in zip(bufs, ms):
        lrun = None
        for j in range(buf.shape[0]):
            e = jnp.exp2(buf[j] - m)
            buf[j] = e
            part = _fold_rows(e, jnp.sum)
            lrun = part if lrun is None else lrun + part
        ls.append(jnp.sum(lrun, axis=0, keepdims=True))
    return ls


def _diff_substage(q_ref, rows_s, k_ref, bufs_s, bufs_e, ms_e, bufs_c, ls_c, vt_ref, lam_full, tk):
    nchunk = bufs_c[0].shape[0]
    q_s = [q_ref[rows_s, c * DIFF_HEAD_DIM:(c + 1) * DIFF_HEAD_DIM] for c in range(2)]
    rho = lam_full * ls_c[0] / ls_c[1]
    mrun, lrun, acc = [None, None], [None, None], None
    for j in range(nchunk):
        keys = slice(j * tk, (j + 1) * tk)
        for c in range(2):
            cols = slice(c * DIFF_HEAD_DIM, (c + 1) * DIFF_HEAD_DIM)
            s = lax.dot_general(k_ref[keys, cols], q_s[c], (((1,), (1,)), ((), ())),
                                preferred_element_type=F32)
            bufs_s[c][j] = s
            part = _fold_rows(s, jnp.max)
            mrun[c] = part if mrun[c] is None else jnp.maximum(mrun[c], part)
        for c in range(2):
            e = jnp.exp2(bufs_e[c][j] - ms_e[c])
            bufs_e[c][j] = e
            part = _fold_rows(e, jnp.sum)
            lrun[c] = part if lrun[c] is None else lrun[c] + part
        a = bufs_c[0][j] - bufs_c[1][j] * rho
        d = jnp.dot(vt_ref[:, keys], a.astype(BF16), preferred_element_type=F32)
        acc = d if acc is None else acc + d
    ms_s = [jnp.max(m, axis=0, keepdims=True) for m in mrun]
    ls_e = [jnp.sum(l, axis=0, keepdims=True) for l in lrun]
    return ms_s, ls_e, acc * (1.0 / ls_c[0])


def _diff_attn_kernel(lam_ref, gsub_ref, q_ref, k_ref, v_ref, o_ref, x0_ref, x1_ref, y0_ref, y1_ref,
                      z0_ref, z1_ref, stat_ref, vt_ref, *, tq, tk, lambda_init):
    lam = lam_ref[...]
    lam_full = (jnp.exp(jnp.sum(lam[0:1] * lam[1:2], axis=-1, keepdims=True))
                - jnp.exp(jnp.sum(lam[2:3] * lam[3:4], axis=-1, keepdims=True)) + lambda_init)
    vt_ref[...] = v_ref[...].T
    ntile = q_ref.shape[0] // tq
    bx, by, bz = (x0_ref, x1_ref), (y0_ref, y1_ref), (z0_ref, z1_ref)

    def rows(t):
        return pl.ds(pl.multiple_of(t * tq, tq), tq)

    def emit(acc, t):
        ms_o = jnp.sum(acc * acc, axis=0, keepdims=True) * (1.0 / (2 * DIFF_HEAD_DIM))
        y = acc * lax.rsqrt(ms_o + EPS) * gsub_ref[...] * (1.0 - lambda_init)
        o_ref[rows(t), :] = y.T.astype(o_ref.dtype)

    ls_x = _diff_numerators(bx, _diff_scores(q_ref, slice(0, tq), k_ref, bx, tk))
    ms_y = _diff_scores(q_ref, slice(tq, 2 * tq), k_ref, by, tk)
    for r, v in enumerate(ls_x + ms_y):
        stat_ref[r:r + 1, :] = v

    def triple(i, carry):
        t = 3 * i
        ls_x = [stat_ref[0:1, :], stat_ref[1:2, :]]
        ms_y = [stat_ref[2:3, :], stat_ref[3:4, :]]
        ms_z, ls_y, acc = _diff_substage(q_ref, rows(t + 2), k_ref, bz, by, ms_y, bx, ls_x, vt_ref, lam_full, tk)
        emit(acc, t)
        ms_x, ls_z, acc = _diff_substage(q_ref, rows(t + 3), k_ref, bx, bz, ms_z, by, ls_y, vt_ref, lam_full, tk)
        emit(acc, t + 1)
        ms_y, ls_x, acc = _diff_substage(q_ref, rows(jnp.minimum(t + 4, ntile - 1)), k_ref, by, bx, ms_x, bz, ls_z,
                                         vt_ref, lam_full, tk)
        emit(acc, t + 2)
        for r, v in enumerate(ls_x + ms_y):
            stat_ref[r:r + 1, :] = v
        return carry

    lax.fori_loop(0, (ntile - 1) // 3, triple, 0)
    rho = lam_full * stat_ref[0:1, :] / stat_ref[1:2, :]
    acc = None
    for j in range(bx[0].shape[0]):
        a = bx[0][j] - bx[1][j] * rho
        d = jnp.dot(vt_ref[:, j * tk:(j + 1) * tk], a.astype(BF16), preferred_element_type=F32)
        acc = d if acc is None else acc + d
    emit(acc * (1.0 / stat_ref[0:1, :]), ntile - 1)


def _diff_attention(qk, v, lam, g_sub, lambda_init, batch, seq):
    tq, tk = DIFF_TQ, ATT_TK
    hw = 2 * DIFF_HEAD_DIM
    assert (seq // tq - 1) % 3 == 0 and seq // tq >= 4
    blocks = [((seq, hw), BF16)] * 4 + [((hw, tq), F32)]
    scratch = [((seq // tk, tk, tq), F32)] * 6 + [((4, tq), F32), ((hw, seq), BF16)]
    return pl.pallas_call(
        functools.partial(_diff_attn_kernel, tq=tq, tk=tk, lambda_init=lambda_init),
        grid=(batch, DIFF_HEADS),
        in_specs=[pl.BlockSpec((4, DIFF_HEAD_DIM), lambda b, h: (0, 0)),
                  pl.BlockSpec((hw, tq), lambda b, h: (0, 0)),
                  pl.BlockSpec((seq, hw), lambda b, h: (b, h)),
                  pl.BlockSpec((seq, hw), lambda b, h: (b, DIFF_HEADS + h)),
                  pl.BlockSpec((seq, hw), lambda b, h: (b, h))],
        out_specs=pl.BlockSpec((seq, hw), lambda b, h: (b, h)),
        out_shape=jax.ShapeDtypeStruct((batch * seq, DIFF_V_WIDTH), BF16),
        scratch_shapes=[pltpu.VMEM(s, d) for s, d in scratch],
        compiler_params=pltpu.CompilerParams(
            dimension_semantics=("arbitrary", "arbitrary"),
            vmem_limit_bytes=_vmem_limit(blocks, scratch)),
        name="diff_attention",
    )(lam, jnp.broadcast_to(g_sub[:, None], (hw, tq)), qk, qk, v)


def _score_sweep(q, k_ref, s_ref, tk):
    mpart = None
    for j in range(s_ref.shape[0]):
        s = lax.dot_general(q, k_ref[j * tk:(j + 1) * tk, :], (((1,), (1,)), ((), ())),
                            preferred_element_type=F32)
        s_ref[j] = s
        for g in range(tk // LANES):
            sg = s[:, g * LANES:(g + 1) * LANES]
            mpart = sg if mpart is None else jnp.maximum(mpart, sg)
    return jnp.max(mpart, axis=-1, keepdims=True)


def _fused_sweep(q_next, k_ref, s_next_ref, s_ref, m, v_ref, tk):
    mpart, acc = None, None
    for j in range(s_ref.shape[0]):
        s = lax.dot_general(q_next, k_ref[j * tk:(j + 1) * tk, :], (((1,), (1,)), ((), ())),
                            preferred_element_type=F32)
        s_next_ref[j] = s
        e = jnp.exp2(s_ref[j] - m)
        for g in range(tk // LANES):
            sg = s[:, g * LANES:(g + 1) * LANES]
            mpart = sg if mpart is None else jnp.maximum(mpart, sg)
        d = jnp.dot(e.astype(BF16), v_ref[j * tk:(j + 1) * tk, :], preferred_element_type=F32)
        acc = d if acc is None else acc + d
    return jnp.max(mpart, axis=-1, keepdims=True), acc


def _mla_attn_kernel(q_ref, k_ref, v_ref, o_ref, s0_ref, s1_ref, m0_ref, *, tq, tk):
    npair = q_ref.shape[0] // (2 * tq)
    m0_ref[...] = _score_sweep(q_ref[0:tq, :], k_ref, s0_ref, tk)

    def finish(acc, rows):
        o_ref[rows, :] = (acc[:, :MLA_V] / acc[:, MLA_V:MLA_V + 1]).astype(o_ref.dtype)

    def tiles(t):
        return pl.ds(pl.multiple_of(t * tq, tq), tq)

    def pair(i, carry):
        m1, acc = _fused_sweep(q_ref[tiles(2 * i + 1), :], k_ref, s1_ref, s0_ref, m0_ref[...], v_ref, tk)
        finish(acc, tiles(2 * i))
        m0, acc = _fused_sweep(q_ref[tiles(2 * i + 2), :], k_ref, s0_ref, s1_ref, m1, v_ref, tk)
        m0_ref[...] = m0
        finish(acc, tiles(2 * i + 1))
        return carry

    lax.fori_loop(0, npair - 1, pair, 0)
    last = 2 * (npair - 1)
    m1, acc = _fused_sweep(q_ref[(last + 1) * tq:(last + 2) * tq, :], k_ref, s1_ref, s0_ref, m0_ref[...], v_ref, tk)
    finish(acc, slice(last * tq, (last + 1) * tq))
    acc = None
    for j in range(s1_ref.shape[0]):
        e = jnp.exp2(s1_ref[j] - m1)
        d = jnp.dot(e.astype(BF16), v_ref[j * tk:(j + 1) * tk, :], preferred_element_type=F32)
        acc = d if acc is None else acc + d
    finish(acc, slice((last + 1) * tq, (last + 2) * tq))


def _mla_attention(q, k, v, batch, seq):
    tq, tk = ATT_TQ, MLA_TK
    blocks = [((seq, MLA_QK_PAD), BF16)] * 2 + [((seq, MLA_V_PAD), BF16), ((seq, MLA_V), BF16)]
    scratch = [((seq // tk, tq, tk), F32)] * 2 + [((tq, 1), F32)]
    return pl.pallas_call(
        functools.partial(_mla_attn_kernel, tq=tq, tk=tk),
        grid=(batch, MLA_HEADS),
        in_specs=[pl.BlockSpec((seq, MLA_QK_PAD), lambda b, h: (b, h)),
                  pl.BlockSpec((seq, MLA_QK_PAD), lambda b, h: (b, h)),
                  pl.BlockSpec((seq, MLA_V_PAD), lambda b, h: (b, h))],
        out_specs=pl.BlockSpec((seq, MLA_V), lambda b, h: (b, h)),
        out_shape=jax.ShapeDtypeStruct((batch * seq, MLA_HEADS * MLA_V), BF16),
        scratch_shapes=[pltpu.VMEM(s, d) for s, d in scratch],
        compiler_params=pltpu.CompilerParams(
            dimension_semantics=("arbitrary", "arbitrary"),
            vmem_limit_bytes=_vmem_limit(blocks, scratch)),
        name="mla_attention",
    )(q, k, v)


def _row_spec(cols):
    return (TM, cols), (lambda i, j, k: (i, 0))


def _lane_gain(g):
    return jnp.broadcast_to(g[:, None], (g.shape[0], LANES))


def _diff_mixer(stream, g_mix, w_in, w_out, layer, g_q, g_k, lam, g_sub, tabs, lambda_init, batch, seq):
    x, xb, ssq = stream
    qscale = DIFF_HEAD_DIM ** -0.5 * LOG2E
    gain = jnp.concatenate([jnp.tile(g_q * qscale, 2 * DIFF_HEADS), jnp.tile(g_k, 2 * DIFF_HEADS)])[None, :]
    tab_extras = [(t, *_row_spec(LANES)) for t in tabs]
    g_rows = _lane_gain(g_mix)
    (qk,) = _matmul(xb, w_in, layer=layer, cols=(0, 2 * DIFF_QK_WIDTH), tm=TM, tn=TN, tk=D_MODEL,
                    norm=ssq, gain=g_rows, epilogue=_ep_diff_qk,
                    extras=[(gain, (1, TN), lambda i, j, k: (0, j))] + tab_extras,
                    outs=[(BF16, 2 * DIFF_QK_WIDTH, TN)], name="diff_qk_proj")
    (v,) = _matmul(xb, w_in, layer=layer, cols=(2 * DIFF_QK_WIDTH, DIFF_V_WIDTH), tm=TM, tn=TN, tk=D_MODEL,
                   norm=ssq, gain=g_rows, epilogue=_ep_store, outs=[(BF16, DIFF_V_WIDTH, TN)],
                   name="diff_v_proj")
    o = _diff_attention(qk, v, lam, g_sub, lambda_init, batch, seq)
    return _matmul(o, w_out, layer=layer, tm=TM, tn=TN, tk=DIFF_V_WIDTH, epilogue=_ep_resid, stats=True,
                   extras=[(x, (TM, TN), lambda i, j, k: (i, j))],
                   outs=[(F32, D_MODEL, TN)], name="diff_out_proj")


def _mla_mixer(stream, g_mix, w_in, g_cq, g_ckv, w_uq, w_ukv, g_q, g_k, w_out, layer, tabs, batch, seq):
    x, xb, ssq = stream
    tab_extras = [(t, *_row_spec(LANES)) for t in tabs]
    zpad = jnp.zeros((MLA_ROPE,), F32)
    n_in = MLA_Q_RANK + MLA_KV_RANK + LANES
    w_in_p = jnp.pad(w_in * g_mix[:, None], ((0, 0), (0, n_in - w_in.shape[1]))).astype(BF16)
    const = lambda i, j, k: (0, 0)
    tm_in = 512
    cq, ckv, kr = _matmul(
        xb, w_in_p, tm=tm_in, tn=n_in, tk=D_MODEL, norm=ssq, epilogue=_ep_mla_in,
        extras=[(g_cq[None, :], (1, MLA_Q_RANK), const), (g_ckv[None, :], (1, MLA_KV_RANK), const),
                (jnp.concatenate([g_k[MLA_NOPE:], zpad])[None, :], (1, LANES), const)]
               + [(t, (tm_in, LANES), lambda i, j, k: (i, 0)) for t in tabs],
        outs=[(BF16, MLA_Q_RANK, MLA_Q_RANK), (BF16, MLA_KV_RANK, MLA_KV_RANK), (BF16, LANES, LANES)],
        name="mla_in_proj")
    w_uq_h = w_uq.reshape(MLA_Q_RANK, MLA_HEADS, MLA_QK)
    w_uq_p = jnp.pad(w_uq_h, ((0, 0), (0, 0), (0, MLA_QK_PAD - MLA_QK))).reshape(
        MLA_Q_RANK, MLA_HEADS * MLA_QK_PAD).astype(BF16)
    qscale = MLA_QK ** -0.5 * LOG2E
    g_qs = g_q * qscale
    (q,) = _matmul(
        cq, w_uq_p, tm=TM, tn=TN, tk=MLA_Q_RANK, epilogue=_ep_mla_q,
        extras=[(g_qs[None, :MLA_NOPE], (1, LANES), const),
                (jnp.concatenate([g_qs[MLA_NOPE:], zpad])[None, :], (1, LANES), const)] + tab_extras,
        outs=[(BF16, MLA_HEADS * MLA_QK_PAD, TN)], name="mla_q_proj")
    w_ukv_h = w_ukv.reshape(MLA_KV_RANK, MLA_HEADS, MLA_NOPE + MLA_V).astype(BF16)
    w_uk = w_ukv_h[:, :, :MLA_NOPE].reshape(MLA_KV_RANK, MLA_HEADS * MLA_NOPE)
    w_uv = w_ukv_h[:, :, MLA_NOPE:].reshape(MLA_KV_RANK, MLA_HEADS * MLA_V)
    k, v = _matmul(
        ckv, w_uk, tm=TM, tn=TN, tk=MLA_KV_RANK, epilogue=_ep_mla_kv,
        extras=[(g_k[None, :MLA_NOPE], (1, LANES), const),
                (kr, (TM, LANES), lambda i, j, k: (i, 0)),
                (ckv, (TM, MLA_KV_RANK), lambda i, j, k: (i, 0)),
                (w_uv, (MLA_KV_RANK, TN), lambda i, j, k: (0, j))],
        outs=[(BF16, MLA_HEADS * MLA_QK_PAD, 2 * TN), (BF16, MLA_HEADS * MLA_V_PAD, 2 * TN)],
        name="mla_kv_proj")
    o = _mla_attention(q, k, v, batch, seq)
    return _matmul(o, w_out, layer=layer, tm=TM, tn=TN, tk=MLA_HEADS * MLA_V, epilogue=_ep_resid, stats=True,
                   extras=[(x, (TM, TN), lambda i, j, k: (i, j))],
                   outs=[(F32, D_MODEL, TN)], name="mla_out_proj")


def kernel(x, p, positions, g_mix, g_mlp, g_ple, w1, w2, w_gate, w_ple, diff_w_in, diff_w_out, diff_g_q, diff_g_k, diff_lambda, diff_g_sub, mla_w_in, mla_g_cq, mla_g_ckv, mla_w_uq, mla_w_ukv, mla_g_q, mla_g_k, mla_w_out):
    batch, seq, d = x.shape
    t = batch * seq
    x = x.reshape(t, d)
    pos = positions.reshape(t, 1).astype(F32)
    tabs_d = _rope_tables(pos, DIFF_ROT, keep_rest=True)
    tabs_m = _rope_tables(pos, MLA_ROPE, keep_rest=False)
    p_bf = p.reshape(DEPTH, t, PLE_DIM).astype(BF16)
    stream = (x, *_stream_stats(x))
    for i in range(DEPTH):
        j = i // N_MIXERS
        if i % N_MIXERS == 0:
            lambda_init = 0.8 - 0.6 * math.exp(-0.3 * i)
            stream = _diff_mixer(stream, g_mix[i], diff_w_in, diff_w_out, j, diff_g_q[j], diff_g_k[j],
                                 diff_lambda[j], diff_g_sub[j], tabs_d, lambda_init, batch, seq)
        else:
            stream = _mla_mixer(stream, g_mix[i], mla_w_in[j], mla_g_cq[j], mla_g_ckv[j], mla_w_uq[j],
                                mla_w_ukv[j], mla_g_q[j], mla_g_k[j], mla_w_out, j, tabs_m, batch, seq)
        x, xb, ssq = stream
        (a,) = _matmul(xb, w1, layer=i, tm=TM, tn=TN, tk=D_MODEL, norm=ssq, gain=_lane_gain(g_mlp[i]),
                       epilogue=_ep_relu2, outs=[(BF16, D_FF, TN)], name="mlp_up")
        x, xb, ssq = _matmul(a, w2, layer=i, tm=TM, tn=2 * TN, tk=2048, epilogue=_ep_resid, stats=True,
                             extras=[(x, (TM, 2 * TN), lambda i_, j_, k_: (i_, j_))],
                             outs=[(F32, D_MODEL, 2 * TN)], name="mlp_down")
        stream = _matmul(xb, w_gate, layer=i, tm=TM, tn=TN, tk=D_MODEL, norm=ssq, gain=_lane_gain(g_ple[i]),
                         epilogue=_ep_gate, stats=i + 1 < DEPTH,
                         extras=[(x, (TM, TN), lambda i_, j_, k_: (i_, j_)),
                                 (p_bf[i], (TM, PLE_DIM), lambda i_, j_, k_: (i_, 0)),
                                 (w_ple[i].astype(BF16), (PLE_DIM, TN), lambda i_, j_, k_: (0, j_))],
                         outs=[(F32, D_MODEL, TN)], name="ple_gate")
    return stream[0].reshape(batch, seq, d)
```

```python
import functools
import math

import jax
import jax.numpy as jnp
from jax import lax
from jax.experimental import pallas as pl
from jax.experimental.pallas import tpu as pltpu

D_MODEL = 4096
DEPTH = 4
N_MIXERS = 2
ROPE_THETA = 500000.0
EPS = 1e-6
D_FF = 4 * D_MODEL
PLE_DIM = 256

DIFF_HEAD_DIM = 128
DIFF_HEADS = D_MODEL // (2 * DIFF_HEAD_DIM)
DIFF_ROT = DIFF_HEAD_DIM // 4
DIFF_QK_WIDTH = DIFF_HEADS * 2 * DIFF_HEAD_DIM
DIFF_V_WIDTH = DIFF_HEADS * 2 * DIFF_HEAD_DIM

MLA_HEADS = 32
MLA_NOPE = 128
MLA_ROPE = 64
MLA_V = 128
MLA_QK = MLA_NOPE + MLA_ROPE
MLA_Q_RANK = 1024
MLA_KV_RANK = 512

LANES = 128
MLA_QK_PAD = 2 * LANES
MLA_V_PAD = 2 * LANES
V7X_VMEM_BYTES = 64 * 1024 * 1024
VMEM_TEMP_ALLOWANCE = 12 * 1024 * 1024
LOG2E = math.log2(math.e)

TM = 1024
TN = 512
MLA_TN = 2048
SUB_ROWS = 256
ATT_TQ = 512
DIFF_TQ = 256
ATT_TK = 1024
MLA_TK = 512

F32 = jnp.float32
BF16 = jnp.bfloat16


def _nbytes(shape, dtype):
    return math.prod(shape) * jnp.dtype(dtype).itemsize


def _vmem_limit(blocks, scratch=()):
    need = 2 * sum(_nbytes(s, d) for s, d in blocks) + sum(_nbytes(s, d) for s, d in scratch)
    return min(need + VMEM_TEMP_ALLOWANCE, V7X_VMEM_BYTES - 4 * 1024 * 1024)


def _rope_table_kernel(pos_ref, inv_ref, c_ref, s1_ref, s2_ref, *, half, keep_rest):
    ang = pos_ref[...] * inv_ref[...]
    lane = lax.broadcasted_iota(jnp.int32, ang.shape, 1)
    c = jnp.cos(ang)
    s = jnp.sin(ang)
    rest = 1.0 if keep_rest else 0.0
    c_ref[...] = jnp.where(lane < 2 * half, c, rest)
    s1_ref[...] = jnp.where((lane >= half) & (lane < 2 * half), s, 0.0)
    s2_ref[...] = jnp.where(lane < half, -s, 0.0)


def _rope_tables(pos_f32, rot_dim, keep_rest):
    t = pos_f32.shape[0]
    half = rot_dim // 2
    inv = ROPE_THETA ** (-jnp.arange(0, rot_dim, 2, dtype=F32) / rot_dim)
    inv_row = jnp.concatenate([inv, inv, jnp.zeros((LANES - rot_dim,), F32)])[None, :]
    tm = 1024
    out = jax.ShapeDtypeStruct((t, LANES), F32)
    spec = pl.BlockSpec((tm, LANES), lambda i: (i, 0))
    return pl.pallas_call(
        functools.partial(_rope_table_kernel, half=half, keep_rest=keep_rest),
        grid=(t // tm,),
        in_specs=[pl.BlockSpec((tm, 1), lambda i: (i, 0)),
                  pl.BlockSpec((1, LANES), lambda i: (0, 0))],
        out_specs=[spec, spec, spec],
        out_shape=[out, out, out],
        name="rope_tables",
    )(pos_f32, inv_row)


def _rope(y, c, s1, s2, half):
    return y * c + pltpu.roll(y, half, 1) * s1 + pltpu.roll(y, LANES - half, 1) * s2


def _rms(x, gain, n):
    ms = jnp.sum(x * x, axis=-1, keepdims=True) * (1.0 / n)
    return x * lax.rsqrt(ms + EPS) * gain


def _stream_stats_kernel(x_ref, xb_ref, ssq_ref):
    x = x_ref[...]
    xb_ref[...] = x.astype(BF16)
    sq = x * x
    part = sq[:, :LANES]
    for c in range(1, sq.shape[1] // LANES):
        part = part + sq[:, c * LANES:(c + 1) * LANES]
    ssq_ref[...] = part


def _stream_stats(x):
    t, d = x.shape
    tm = 256
    return pl.pallas_call(
        _stream_stats_kernel,
        grid=(t // tm,),
        in_specs=[pl.BlockSpec((tm, d), lambda i: (i, 0))],
        out_specs=[pl.BlockSpec((tm, d), lambda i: (i, 0)), pl.BlockSpec((tm, LANES), lambda i: (i, 0))],
        out_shape=[jax.ShapeDtypeStruct((t, d), BF16), jax.ShapeDtypeStruct((t, LANES), F32)],
        compiler_params=pltpu.CompilerParams(
            dimension_semantics=("arbitrary",),
            vmem_limit_bytes=_vmem_limit([((tm, d), F32), ((tm, d), BF16), ((tm, LANES), F32)])),
        name="stream_stats",
    )(x)


def _mm_body(*refs, nk, n_extra, n_out, epilogue, sub, has_norm, has_gain, stats):
    a_ref, w_ref = refs[0], refs[1]
    pos = 2
    ssq_ref = refs[pos] if has_norm else None
    pos += has_norm
    gain_ref = refs[pos] if has_gain else None
    pos += has_gain
    extra = refs[pos:pos + n_extra]
    outs = refs[pos + n_extra:pos + n_extra + n_out]
    tm = a_ref.shape[0]

    if stats:
        @pl.when((pl.program_id(1) == 0) & (pl.program_id(2) == 0))
        def _():
            outs[-1][...] = jnp.zeros_like(outs[-1])

    def weights():
        w = w_ref[...]
        if has_gain:
            g = gain_ref[...]
            w = jnp.concatenate([w[:, c * LANES:(c + 1) * LANES] * g for c in range(w.shape[1] // LANES)], axis=1)
        return w.astype(BF16)

    def scaled(acc, rows):
        if not has_norm:
            return acc
        ms = jnp.sum(ssq_ref[rows, :], axis=-1, keepdims=True) * (1.0 / D_MODEL)
        return acc * lax.rsqrt(ms + EPS)

    if nk == 1:
        w = weights()
        for r in range(tm // sub):
            rows = slice(r * sub, (r + 1) * sub)
            acc = jnp.dot(a_ref[rows, :], w, preferred_element_type=F32)
            epilogue(scaled(acc, rows), extra, outs, rows)
        return
    assert not has_norm
    acc_ref = refs[-1]
    k = pl.program_id(2)

    def partial_dots():
        w = weights()
        for r in range(tm // sub):
            rows = slice(r * sub, (r + 1) * sub)
            yield rows, jnp.dot(a_ref[rows, :], w, preferred_element_type=F32)

    @pl.when(k == 0)
    def _():
        for rows, d in partial_dots():
            acc_ref[rows, :] = d

    @pl.when((k > 0) & (k < nk - 1))
    def _():
        for rows, d in partial_dots():
            acc_ref[rows, :] += d

    @pl.when(k == nk - 1)
    def _():
        for rows, d in partial_dots():
            epilogue(acc_ref[rows, :] + d, extra, outs, rows)


def _matmul(a, w, *, tm, tn, tk, epilogue, extras=(), outs, name, sub=SUB_ROWS, layer=None, cols=None,
            norm=None, gain=None, stats=False):
    m, kdim = a.shape
    col0, n = cols if cols is not None else (0, w.shape[-1])
    nk = kdim // tk
    assert m % tm == 0 and n % tn == 0 and col0 % tn == 0 and kdim % tk == 0 and tm % min(sub, tm) == 0
    j0 = col0 // tn
    if layer is None:
        w_spec = pl.BlockSpec((tk, tn), lambda i, j, k: (k, j0 + j))
    else:
        w_spec = pl.BlockSpec((None, tk, tn), lambda i, j, k: (layer, k, j0 + j))
    in_specs = [pl.BlockSpec((tm, tk), lambda i, j, k: (i, k)), w_spec]
    blocks = [((tm, tk), a.dtype), ((tk, tn), w.dtype)]
    operands = [a, w]
    if norm is not None:
        extras = [(norm, (tm, LANES), lambda i, j, k: (i, 0))] + list(extras)
    if gain is not None:
        pos = 1 if norm is not None else 0
        extras = list(extras[:pos]) + [(gain, (tk, LANES), lambda i, j, k: (k, 0))] + list(extras[pos:])
    for arr, bshape, imap in extras:
        in_specs.append(pl.BlockSpec(bshape, imap))
        blocks.append((bshape, arr.dtype))
        operands.append(arr)
    out_specs, out_shape = [], []
    for dtype, ocols, bcols in outs:
        out_specs.append(pl.BlockSpec((tm, bcols), lambda i, j, k: (i, j)))
        out_shape.append(jax.ShapeDtypeStruct((m, ocols), dtype))
        blocks.append(((tm, bcols), dtype))
    if stats:
        out_specs.append(pl.BlockSpec((tm, tn), lambda i, j, k: (i, j)))
        out_shape.append(jax.ShapeDtypeStruct((m, n), BF16))
        out_specs.append(pl.BlockSpec((tm, LANES), lambda i, j, k: (i, 0)))
        out_shape.append(jax.ShapeDtypeStruct((m, LANES), F32))
        blocks += [((tm, tn), BF16), ((tm, LANES), F32)]
    scratch = [((tm, tn), F32)] if nk > 1 else []
    n_body = (norm is not None) + (gain is not None)
    return pl.pallas_call(
        functools.partial(_mm_body, nk=nk, n_extra=len(extras) - n_body, n_out=len(out_specs),
                          epilogue=epilogue, sub=min(sub, tm), has_norm=norm is not None,
                          has_gain=gain is not None, stats=stats),
        grid=(m // tm, n // tn, nk),
        in_specs=in_specs,
        out_specs=out_specs,
        out_shape=out_shape,
        scratch_shapes=[pltpu.VMEM(s, d) for s, d in scratch],
        compiler_params=pltpu.CompilerParams(
            dimension_semantics=("arbitrary", "arbitrary", "arbitrary"),
            vmem_limit_bytes=_vmem_limit(blocks, scratch)),
        name=name,
    )(*operands)


def _emit_stream(x_new, outs, rows):
    outs[0][rows, :] = x_new
    if len(outs) == 3:
        outs[1][rows, :] = x_new.astype(BF16)
        sq = x_new * x_new
        part = sq[:, :LANES]
        for c in range(1, sq.shape[1] // LANES):
            part = part + sq[:, c * LANES:(c + 1) * LANES]
        outs[2][rows, :] += part


def _ep_store(acc, extra, outs, rows):
    outs[0][rows, :] = acc.astype(outs[0].dtype)


def _ep_resid(acc, extra, outs, rows):
    _emit_stream(extra[0][rows, :] + acc, outs, rows)


def _ep_relu2(acc, extra, outs, rows):
    r = jnp.maximum(acc, 0.0)
    outs[0][rows, :] = (r * r).astype(outs[0].dtype)


def _ep_gate(acc, extra, outs, rows):
    res_ref, p_ref, wple_ref = extra
    ple = jnp.dot(p_ref[rows, :], wple_ref[...], preferred_element_type=F32)
    gate = jax.nn.sigmoid(acc)
    _emit_stream(res_ref[rows, :] + gate * ple, outs, rows)


def _ep_diff_qk(acc, extra, outs, rows):
    gain_ref, c_ref, s1_ref, s2_ref = extra
    c, s1, s2 = c_ref[rows, :], s1_ref[rows, :], s2_ref[rows, :]
    for g in range(acc.shape[1] // LANES):
        sl = slice(g * LANES, (g + 1) * LANES)
        y = _rms(acc[:, sl], gain_ref[:, sl], DIFF_HEAD_DIM)
        outs[0][rows, sl] = _rope(y, c, s1, s2, DIFF_ROT // 2).astype(outs[0].dtype)


def _ep_mla_in(acc, extra, outs, rows):
    gcq_ref, gckv_ref, gkr_ref, c_ref, s1_ref, s2_ref = extra
    cq_ref, ckv_ref, kr_ref = outs
    cq_ref[rows, :] = _rms(acc[:, :MLA_Q_RANK], gcq_ref[...], MLA_Q_RANK).astype(cq_ref.dtype)
    lo = MLA_Q_RANK
    ckv_ref[rows, :] = _rms(acc[:, lo:lo + MLA_KV_RANK], gckv_ref[...], MLA_KV_RANK).astype(ckv_ref.dtype)
    lo += MLA_KV_RANK
    y = _rms(acc[:, lo:lo + LANES], gkr_ref[...], MLA_ROPE)
    kr_ref[rows, :] = _rope(y, c_ref[rows, :], s1_ref[rows, :], s2_ref[rows, :],
                            MLA_ROPE // 2).astype(kr_ref.dtype)


def _ep_mla_q(acc, extra, outs, rows):
    gn_ref, gr_ref, c_ref, s1_ref, s2_ref = extra
    c, s1, s2 = c_ref[rows, :], s1_ref[rows, :], s2_ref[rows, :]
    for h in range(acc.shape[1] // MLA_QK_PAD):
        lo = h * MLA_QK_PAD
        yn = _rms(acc[:, lo:lo + LANES], gn_ref[...], MLA_NOPE)
        outs[0][rows, lo:lo + LANES] = yn.astype(outs[0].dtype)
        yr = _rms(acc[:, lo + LANES:lo + 2 * LANES], gr_ref[...], MLA_ROPE)
        outs[0][rows, lo + LANES:lo + 2 * LANES] = _rope(yr, c, s1, s2, MLA_ROPE // 2).astype(outs[0].dtype)


def _ep_mla_kv(acc, extra, outs, rows):
    gk_ref, kr_ref, a_ref, wuv_ref = extra
    k_ref, v_ref = outs
    kr = kr_ref[rows, :]
    v = jnp.dot(a_ref[rows, :], wuv_ref[...], preferred_element_type=F32).astype(v_ref.dtype)
    ones = jnp.ones((acc.shape[0], LANES), v_ref.dtype)
    for h in range(acc.shape[1] // LANES):
        yk = _rms(acc[:, h * LANES:(h + 1) * LANES], gk_ref[...], MLA_NOPE)
        lo = h * MLA_QK_PAD
        k_ref[rows, lo:lo + LANES] = yk.astype(k_ref.dtype)
        k_ref[rows, lo + LANES:lo + 2 * LANES] = kr
        lo = h * MLA_V_PAD
        v_ref[rows, lo:lo + LANES] = v[:, h * LANES:(h + 1) * LANES]
        v_ref[rows, lo + LANES:lo + 2 * LANES] = ones


SUBLANE_PARTIALS = 32


def _fold_rows(x, op):
    r = x.shape[0] // SUBLANE_PARTIALS
    return op(x.reshape(r, SUBLANE_PARTIALS, x.shape[1]), axis=0)


def _score_sweep_t(q, k_ref, k_cols, s_ref, tk):
    mrun = None
    for j in range(s_ref.shape[0]):
        s = lax.dot_general(k_ref[j * tk:(j + 1) * tk, k_cols], q, (((1,), (1,)), ((), ())),
                            preferred_element_type=F32)
        s_ref[j] = s
        part = _fold_rows(s, jnp.max)
        mrun = part if mrun is None else jnp.maximum(mrun, part)
    return jnp.max(mrun, axis=0, keepdims=True)


def _diff_scores(q_ref, rows, k_ref, bufs, tk):
    ms = []
    for c, buf in enumerate(bufs):
        cols = slice(c * DIFF_HEAD_DIM, (c + 1) * DIFF_HEAD_DIM)
        ms.append(_score_sweep_t(q_ref[rows, cols], k_ref, cols, buf, tk))
    return ms


def _diff_numerators(bufs, ms):
    ls = []
    for buf, m in zip(bufs, ms):
        lrun = None
        for j in range(buf.shape[0]):
            e = jnp.exp2(buf[j] - m)
            buf[j] = e
            part = _fold_rows(e, jnp.sum)
            lrun = part if lrun is None else lrun + part
        ls.append(jnp.sum(lrun, axis=0, keepdims=True))
    return ls


def _diff_substage(q_ref, rows_s, k_ref, bufs_s, bufs_e, ms_e, bufs_c, ls_c, vt_ref, lam_full, tk):
    nchunk = bufs_c[0].shape[0]
    q_s = [q_ref[rows_s, c * DIFF_HEAD_DIM:(c + 1) * DIFF_HEAD_DIM] for c in range(2)]
    rho = lam_full * ls_c[0] / ls_c[1]
    mrun, lrun, acc = [None, None], [None, None], None
    for j in range(nchunk):
        keys = slice(j * tk, (j + 1) * tk)
        for c in range(2):
            cols = slice(c * DIFF_HEAD_DIM, (c + 1) * DIFF_HEAD_DIM)
            s = lax.dot_general(k_ref[keys, cols], q_s[c], (((1,), (1,)), ((), ())),
                                preferred_element_type=F32)
            bufs_s[c][j] = s
            part = _fold_rows(s, jnp.max)
            mrun[c] = part if mrun[c] is None else jnp.maximum(mrun[c], part)
        for c in range(2):
            e = jnp.exp2(bufs_e[c][j] - ms_e[c])
            bufs_e[c][j] = e
            part = _fold_rows(e, jnp.sum)
            lrun[c] = part if lrun[c] is None else lrun[c] + part
        a = bufs_c[0][j] - bufs_c[1][j] * rho
        d = jnp.dot(vt_ref[:, keys], a.astype(BF16), preferred_element_type=F32)
        acc = d if acc is None else acc + d
    ms_s = [jnp.max(m, axis=0, keepdims=True) for m in mrun]
    ls_e = [jnp.sum(l, axis=0, keepdims=True) for l in lrun]
    return ms_s, ls_e, acc * (1.0 / ls_c[0])


def _diff_attn_kernel(lam_ref, gsub_ref, q_ref, k_ref, v_ref, o_ref, x0_ref, x1_ref, y0_ref, y1_ref,
                      z0_ref, z1_ref, stat_ref, vt_ref, *, tq, tk, lambda_init):
    lam = lam_ref[...]
    lam_full = (jnp.exp(jnp.sum(lam[0:1] * lam[1:2], axis=-1, keepdims=True))
                - jnp.exp(jnp.sum(lam[2:3] * lam[3:4], axis=-1, keepdims=True)) + lambda_init)
    vt_ref[...] = v_ref[...].T
    ntile = q_ref.shape[0] // tq
    bx, by, bz = (x0_ref, x1_ref), (y0_ref, y1_ref), (z0_ref, z1_ref)

    def rows(t):
        return pl.ds(pl.multiple_of(t * tq, tq), tq)

    def emit(acc, t):
        ms_o = jnp.sum(acc * acc, axis=0, keepdims=True) * (1.0 / (2 * DIFF_HEAD_DIM))
        y = acc * lax.rsqrt(ms_o + EPS) * gsub_ref[...] * (1.0 - lambda_init)
        o_ref[rows(t), :] = y.T.astype(o_ref.dtype)

    ls_x = _diff_numerators(bx, _diff_scores(q_ref, slice(0, tq), k_ref, bx, tk))
    ms_y = _diff_scores(q_ref, slice(tq, 2 * tq), k_ref, by, tk)
    for r, v in enumerate(ls_x + ms_y):
        stat_ref[r:r + 1, :] = v

    def triple(i, carry):
        t = 3 * i
        ls_x = [stat_ref[0:1, :], stat_ref[1:2, :]]
        ms_y = [stat_ref[2:3, :], stat_ref[3:4, :]]
        ms_z, ls_y, acc = _diff_substage(q_ref, rows(t + 2), k_ref, bz, by, ms_y, bx, ls_x, vt_ref, lam_full, tk)
        emit(acc, t)
        ms_x, ls_z, acc = _diff_substage(q_ref, rows(t + 3), k_ref, bx, bz, ms_z, by, ls_y, vt_ref, lam_full, tk)
        emit(acc, t + 1)
        ms_y, ls_x, acc = _diff_substage(q_ref, rows(jnp.minimum(t + 4, ntile - 1)), k_ref, by, bx, ms_x, bz, ls_z,
                                         vt_ref, lam_full, tk)
        emit(acc, t + 2)
        for r, v in enumerate(ls_x + ms_y):
            stat_ref[r:r + 1, :] = v
        return carry

    lax.fori_loop(0, (ntile - 1) // 3, triple, 0)
    rho = lam_full * stat_ref[0:1, :] / stat_ref[1:2, :]
    acc = None
    for j in range(bx[0].shape[0]):
        a = bx[0][j] - bx[1][j] * rho
        d = jnp.dot(vt_ref[:, j * tk:(j + 1) * tk], a.astype(BF16), preferred_element_type=F32)
        acc = d if acc is None else acc + d
    emit(acc * (1.0 / stat_ref[0:1, :]), ntile - 1)


def _diff_attention(qk, v, lam, g_sub, lambda_init, batch, seq):
    tq, tk = DIFF_TQ, ATT_TK
    hw = 2 * DIFF_HEAD_DIM
    assert (seq // tq - 1) % 3 == 0 and seq // tq >= 4
    blocks = [((seq, hw), BF16)] * 4 + [((hw, tq), F32)]
    scratch = [((seq // tk, tk, tq), F32)] * 6 + [((4, tq), F32), ((hw, seq), BF16)]
    return pl.pallas_call(
        functools.partial(_diff_attn_kernel, tq=tq, tk=tk, lambda_init=lambda_init),
        grid=(batch, DIFF_HEADS),
        in_specs=[pl.BlockSpec((4, DIFF_HEAD_DIM), lambda b, h: (0, 0)),
                  pl.BlockSpec((hw, tq), lambda b, h: (0, 0)),
                  pl.BlockSpec((seq, hw), lambda b, h: (b, h)),
                  pl.BlockSpec((seq, hw), lambda b, h: (b, DIFF_HEADS + h)),
                  pl.BlockSpec((seq, hw), lambda b, h: (b, h))],
        out_specs=pl.BlockSpec((seq, hw), lambda b, h: (b, h)),
        out_shape=jax.ShapeDtypeStruct((batch * seq, DIFF_V_WIDTH), BF16),
        scratch_shapes=[pltpu.VMEM(s, d) for s, d in scratch],
        compiler_params=pltpu.CompilerParams(
            dimension_semantics=("arbitrary", "arbitrary"),
            vmem_limit_bytes=_vmem_limit(blocks, scratch)),
        name="diff_attention",
    )(lam, jnp.broadcast_to(g_sub[:, None], (hw, tq)), qk, qk, v)


def _score_sweep(q, k_ref, s_ref, tk):
    mpart = None
    for j in range(s_ref.shape[0]):
        s = lax.dot_general(q, k_ref[j * tk:(j + 1) * tk, :], (((1,), (1,)), ((), ())),
                            preferred_element_type=F32)
        s_ref[j] = s
        for g in range(tk // LANES):
            sg = s[:, g * LANES:(g + 1) * LANES]
            mpart = sg if mpart is None else jnp.maximum(mpart, sg)
    return jnp.max(mpart, axis=-1, keepdims=True)


def _fused_sweep(q_next, k_ref, s_next_ref, s_ref, m, v_ref, tk):
    mpart, acc = None, None
    for j in range(s_ref.shape[0]):
        s = lax.dot_general(q_next, k_ref[j * tk:(j + 1) * tk, :], (((1,), (1,)), ((), ())),
                            preferred_element_type=F32)
        s_next_ref[j] = s
        e = jnp.exp2(s_ref[j] - m)
        for g in range(tk // LANES):
            sg = s[:, g * LANES:(g + 1) * LANES]
            mpart = sg if mpart is None else jnp.maximum(mpart, sg)
        d = jnp.dot(e.astype(BF16), v_ref[j * tk:(j + 1) * tk, :], preferred_element_type=F32)
        acc = d if acc is None else acc + d
    return jnp.max(mpart, axis=-1, keepdims=True), acc


def _mla_attn_kernel(q_ref, k_ref, v_ref, o_ref, s0_ref, s1_ref, m0_ref, *, tq, tk):
    npair = q_ref.shape[0] // (2 * tq)
    m0_ref[...] = _score_sweep(q_ref[0:tq, :], k_ref, s0_ref, tk)

    def finish(acc, rows):
        o_ref[rows, :] = (acc[:, :MLA_V] / acc[:, MLA_V:MLA_V + 1]).astype(o_ref.dtype)

    def tiles(t):
        return pl.ds(pl.multiple_of(t * tq, tq), tq)

    def pair(i, carry):
        m1, acc = _fused_sweep(q_ref[tiles(2 * i + 1), :], k_ref, s1_ref, s0_ref, m0_ref[...], v_ref, tk)
        finish(acc, tiles(2 * i))
        m0, acc = _fused_sweep(q_ref[tiles(2 * i + 2), :], k_ref, s0_ref, s1_ref, m1, v_ref, tk)
        m0_ref[...] = m0
        finish(acc, tiles(2 * i + 1))
        return carry

    lax.fori_loop(0, npair - 1, pair, 0)
    last = 2 * (npair - 1)
    m1, acc = _fused_sweep(q_ref[(last + 1) * tq:(last + 2) * tq, :], k_ref, s1_ref, s0_ref, m0_ref[...], v_ref, tk)
    finish(acc, slice(last * tq, (last + 1) * tq))
    acc = None
    for j in range(s1_ref.shape[0]):
        e = jnp.exp2(s1_ref[j] - m1)
        d = jnp.dot(e.astype(BF16), v_ref[j * tk:(j + 1) * tk, :], preferred_element_type=F32)
        acc = d if acc is None else acc + d
    finish(acc, slice((last + 1) * tq, (last + 2) * tq))


def _mla_attention(q, k, v, batch, seq):
    tq, tk = ATT_TQ, MLA_TK
    blocks = [((seq, MLA_QK_PAD), BF16)] * 2 + [((seq, MLA_V_PAD), BF16), ((seq, MLA_V), BF16)]
    scratch = [((seq // tk, tq, tk), F32)] * 2 + [((tq, 1), F32)]
    return pl.pallas_call(
        functools.partial(_mla_attn_kernel, tq=tq, tk=tk),
        grid=(batch, MLA_HEADS),
        in_specs=[pl.BlockSpec((seq, MLA_QK_PAD), lambda b, h: (b, h)),
                  pl.BlockSpec((seq, MLA_QK_PAD), lambda b, h: (b, h)),
                  pl.BlockSpec((seq, MLA_V_PAD), lambda b, h: (b, h))],
        out_specs=pl.BlockSpec((seq, MLA_V), lambda b, h: (b, h)),
        out_shape=jax.ShapeDtypeStruct((batch * seq, MLA_HEADS * MLA_V), BF16),
        scratch_shapes=[pltpu.VMEM(s, d) for s, d in scratch],
        compiler_params=pltpu.CompilerParams(
            dimension_semantics=("arbitrary", "arbitrary"),
            vmem_limit_bytes=_vmem_limit(blocks, scratch)),
        name="mla_attention",
    )(q, k, v)


def _row_spec(cols):
    return (TM, cols), (lambda i, j, k: (i, 0))


def _lane_gain(g):
    return jnp.broadcast_to(g[:, None], (g.shape[0], LANES))


def _diff_mixer(stream, g_mix, w_in, w_out, layer, g_q, g_k, lam, g_sub, tabs, lambda_init, batch, seq):
    x, xb, ssq = stream
    qscale = DIFF_HEAD_DIM ** -0.5 * LOG2E
    gain = jnp.concatenate([jnp.tile(g_q * qscale, 2 * DIFF_HEADS), jnp.tile(g_k, 2 * DIFF_HEADS)])[None, :]
    tab_extras = [(t, *_row_spec(LANES)) for t in tabs]
    g_rows = _lane_gain(g_mix)
    (qk,) = _matmul(xb, w_in, layer=layer, cols=(0, 2 * DIFF_QK_WIDTH), tm=TM, tn=TN, tk=D_MODEL,
                    norm=ssq, gain=g_rows, epilogue=_ep_diff_qk,
                    extras=[(gain, (1, TN), lambda i, j, k: (0, j))] + tab_extras,
                    outs=[(BF16, 2 * DIFF_QK_WIDTH, TN)], name="diff_qk_proj")
    (v,) = _matmul(xb, w_in, layer=layer, cols=(2 * DIFF_QK_WIDTH, DIFF_V_WIDTH), tm=TM, tn=TN, tk=D_MODEL,
                   norm=ssq, gain=g_rows, epilogue=_ep_store, outs=[(BF16, DIFF_V_WIDTH, TN)],
                   name="diff_v_proj")
    o = _diff_attention(qk, v, lam, g_sub, lambda_init, batch, seq)
    return _matmul(o, w_out, layer=layer, tm=TM, tn=TN, tk=DIFF_V_WIDTH, epilogue=_ep_resid, stats=True,
                   extras=[(x, (TM, TN), lambda i, j, k: (i, j))],
                   outs=[(F32, D_MODEL, TN)], name="diff_out_proj")


def _mla_mixer(stream, g_mix, w_in, g_cq, g_ckv, w_uq, w_ukv, g_q, g_k, w_out, layer, tabs, batch, seq):
    x, xb, ssq = stream
    tab_extras = [(t, *_row_spec(LANES)) for t in tabs]
    zpad = jnp.zeros((MLA_ROPE,), F32)
    n_in = MLA_Q_RANK + MLA_KV_RANK + LANES
    w_in_p = jnp.pad(w_in * g_mix[:, None], ((0, 0), (0, n_in - w_in.shape[1]))).astype(BF16)
    const = lambda i, j, k: (0, 0)
    tm_in = 512
    cq, ckv, kr = _matmul(
        xb, w_in_p, tm=tm_in, tn=n_in, tk=D_MODEL, norm=ssq, epilogue=_ep_mla_in,
        extras=[(g_cq[None, :], (1, MLA_Q_RANK), const), (g_ckv[None, :], (1, MLA_KV_RANK), const),
                (jnp.concatenate([g_k[MLA_NOPE:], zpad])[None, :], (1, LANES), const)]
               + [(t, (tm_in, LANES), lambda i, j, k: (i, 0)) for t in tabs],
        outs=[(BF16, MLA_Q_RANK, MLA_Q_RANK), (BF16, MLA_KV_RANK, MLA_KV_RANK), (BF16, LANES, LANES)],
        name="mla_in_proj")
    w_uq_h = w_uq.reshape(MLA_Q_RANK, MLA_HEADS, MLA_QK)
    w_uq_p = jnp.pad(w_uq_h, ((0, 0), (0, 0), (0, MLA_QK_PAD - MLA_QK))).reshape(
        MLA_Q_RANK, MLA_HEADS * MLA_QK_PAD).astype(BF16)
    qscale = MLA_QK ** -0.5 * LOG2E
    g_qs = g_q * qscale
    (q,) = _matmul(
        cq, w_uq_p, tm=TM, tn=MLA_TN, tk=MLA_Q_RANK, epilogue=_ep_mla_q,
        extras=[(g_qs[None, :MLA_NOPE], (1, LANES), const),
                (jnp.concatenate([g_qs[MLA_NOPE:], zpad])[None, :], (1, LANES), const)] + tab_extras,
        outs=[(BF16, MLA_HEADS * MLA_QK_PAD, MLA_TN)], name="mla_q_proj")
    w_ukv_h = w_ukv.reshape(MLA_KV_RANK, MLA_HEADS, MLA_NOPE + MLA_V).astype(BF16)
    w_uk = w_ukv_h[:, :, :MLA_NOPE].reshape(MLA_KV_RANK, MLA_HEADS * MLA_NOPE)
    w_uv = w_ukv_h[:, :, MLA_NOPE:].reshape(MLA_KV_RANK, MLA_HEADS * MLA_V)
    k, v = _matmul(
        ckv, w_uk, tm=TM, tn=MLA_TN // 2, tk=MLA_KV_RANK, epilogue=_ep_mla_kv,
        extras=[(g_k[None, :MLA_NOPE], (1, LANES), const),
                (kr, (TM, LANES), lambda i, j, k: (i, 0)),
                (ckv, (TM, MLA_KV_RANK), lambda i, j, k: (i, 0)),
                (w_uv, (MLA_KV_RANK, MLA_TN // 2), lambda i, j, k: (0, j))],
        outs=[(BF16, MLA_HEADS * MLA_QK_PAD, MLA_TN), (BF16, MLA_HEADS * MLA_V_PAD, MLA_TN)],
        name="mla_kv_proj")
    o = _mla_attention(q, k, v, batch, seq)
    return _matmul(o, w_out, layer=layer, tm=TM, tn=TN, tk=MLA_HEADS * MLA_V, epilogue=_ep_resid, stats=True,
                   extras=[(x, (TM, TN), lambda i, j, k: (i, j))],
                   outs=[(F32, D_MODEL, TN)], name="mla_out_proj")


def kernel(x, p, positions, g_mix, g_mlp, g_ple, w1, w2, w_gate, w_ple, diff_w_in, diff_w_out, diff_g_q, diff_g_k, diff_lambda, diff_g_sub, mla_w_in, mla_g_cq, mla_g_ckv, mla_w_uq, mla_w_ukv, mla_g_q, mla_g_k, mla_w_out):
    batch, seq, d = x.shape
    t = batch * seq
    x = x.reshape(t, d)
    pos = positions.reshape(t, 1).astype(F32)
    tabs_d = _rope_tables(pos, DIFF_ROT, keep_rest=True)
    tabs_m = _rope_tables(pos, MLA_ROPE, keep_rest=False)
    p_bf = p.reshape(DEPTH, t, PLE_DIM).astype(BF16)
    stream = (x, *_stream_stats(x))
    for i in range(DEPTH):
        j = i // N_MIXERS
        if i % N_MIXERS == 0:
            lambda_init = 0.8 - 0.6 * math.exp(-0.3 * i)
            stream = _diff_mixer(stream, g_mix[i], diff_w_in, diff_w_out, j, diff_g_q[j], diff_g_k[j],
                                 diff_lambda[j], diff_g_sub[j], tabs_d, lambda_init, batch, seq)
        else:
            stream = _mla_mixer(stream, g_mix[i], mla_w_in[j], mla_g_cq[j], mla_g_ckv[j], mla_w_uq[j],
                                mla_w_ukv[j], mla_g_q[j], mla_g_k[j], mla_w_out, j, tabs_m, batch, seq)
        x, xb, ssq = stream
        (a,) = _matmul(xb, w1, layer=i, tm=TM, tn=TN, tk=D_MODEL, norm=ssq, gain=_lane_gain(g_mlp[i]),
                       epilogue=_ep_relu2, outs=[(BF16, D_FF, TN)], name="mlp_up")
        x, xb, ssq = _matmul(a, w2, layer=i, tm=TM, tn=2 * TN, tk=2048, epilogue=_ep_resid, stats=True,
                             extras=[(x, (TM, 2 * TN), lambda i_, j_, k_: (i_, j_))],
                             outs=[(F32, D_MODEL, 2 * TN)], name="mlp_down")
        stream = _matmul(xb, w_gate, layer=i, tm=TM, tn=TN, tk=D_MODEL, norm=ssq, gain=_lane_gain(g_ple[i]),
                         epilogue=_ep_gate, stats=i + 1 < DEPTH,
                         extras=[(x, (TM, TN), lambda i_, j_, k_: (i_, j_)),
                                 (p_bf[i], (TM, PLE_DIM), lambda i_, j_, k_: (i_, 0)),
                                 (w_ple[i].astype(BF16), (PLE_DIM, TN), lambda i_, j_, k_: (0, j_))],
                         outs=[(F32, D_MODEL, TN)], name="ple_gate")
    return stream[0].reshape(batch, seq, d)
```

```python
import functools
import math

import jax
import jax.numpy as jnp
from jax import lax
from jax.experimental import pallas as pl
from jax.experimental.pallas import tpu as pltpu

D_MODEL = 4096
DEPTH = 4
N_MIXERS = 2
ROPE_THETA = 500000.0
EPS = 1e-6
D_FF = 4 * D_MODEL
PLE_DIM = 256

DIFF_HEAD_DIM = 128
DIFF_HEADS = D_MODEL // (2 * DIFF_HEAD_DIM)
DIFF_ROT = DIFF_HEAD_DIM // 4
DIFF_QK_WIDTH = DIFF_HEADS * 2 * DIFF_HEAD_DIM
DIFF_V_WIDTH = DIFF_HEADS * 2 * DIFF_HEAD_DIM

MLA_HEADS = 32
MLA_NOPE = 128
MLA_ROPE = 64
MLA_V = 128
MLA_QK = MLA_NOPE + MLA_ROPE
MLA_Q_RANK = 1024
MLA_KV_RANK = 512

LANES = 128
MLA_QK_PAD = 2 * LANES
MLA_V_PAD = 2 * LANES
V7X_VMEM_BYTES = 64 * 1024 * 1024
VMEM_TEMP_ALLOWANCE = 12 * 1024 * 1024
LOG2E = math.log2(math.e)

TM = 1024
TN = 512
MLA_TN = 2048
SUB_ROWS = 256
ATT_TQ = 512
DIFF_TQ = 256
ATT_TK = 1024
MLA_TK = 512

F32 = jnp.float32
BF16 = jnp.bfloat16


def _nbytes(shape, dtype):
    return math.prod(shape) * jnp.dtype(dtype).itemsize


def _vmem_limit(blocks, scratch=()):
    need = 2 * sum(_nbytes(s, d) for s, d in blocks) + sum(_nbytes(s, d) for s, d in scratch)
    return min(need + VMEM_TEMP_ALLOWANCE, V7X_VMEM_BYTES - 4 * 1024 * 1024)


def _rope_table_kernel(pos_ref, inv_ref, c_ref, s1_ref, s2_ref, *, half, keep_rest):
    ang = pos_ref[...] * inv_ref[...]
    lane = lax.broadcasted_iota(jnp.int32, ang.shape, 1)
    c = jnp.cos(ang)
    s = jnp.sin(ang)
    rest = 1.0 if keep_rest else 0.0
    c_ref[...] = jnp.where(lane < 2 * half, c, rest)
    s1_ref[...] = jnp.where((lane >= half) & (lane < 2 * half), s, 0.0)
    s2_ref[...] = jnp.where(lane < half, -s, 0.0)


def _rope_tables(pos_f32, rot_dim, keep_rest):
    t = pos_f32.shape[0]
    half = rot_dim // 2
    inv = ROPE_THETA ** (-jnp.arange(0, rot_dim, 2, dtype=F32) / rot_dim)
    inv_row = jnp.concatenate([inv, inv, jnp.zeros((LANES - rot_dim,), F32)])[None, :]
    tm = 1024
    out = jax.ShapeDtypeStruct((t, LANES), F32)
    spec = pl.BlockSpec((tm, LANES), lambda i: (i, 0))
    return pl.pallas_call(
        functools.partial(_rope_table_kernel, half=half, keep_rest=keep_rest),
        grid=(t // tm,),
        in_specs=[pl.BlockSpec((tm, 1), lambda i: (i, 0)),
                  pl.BlockSpec((1, LANES), lambda i: (0, 0))],
        out_specs=[spec, spec, spec],
        out_shape=[out, out, out],
        name="rope_tables",
    )(pos_f32, inv_row)


def _rope(y, c, s1, s2, half):
    return y * c + pltpu.roll(y, half, 1) * s1 + pltpu.roll(y, LANES - half, 1) * s2


def _rms(x, gain, n):
    ms = jnp.sum(x * x, axis=-1, keepdims=True) * (1.0 / n)
    return x * lax.rsqrt(ms + EPS) * gain


def _stream_stats_kernel(x_ref, xb_ref, ssq_ref):
    x = x_ref[...]
    xb_ref[...] = x.astype(BF16)
    sq = x * x
    part = sq[:, :LANES]
    for c in range(1, sq.shape[1] // LANES):
        part = part + sq[:, c * LANES:(c + 1) * LANES]
    ssq_ref[...] = part


def _stream_stats(x):
    t, d = x.shape
    tm = 256
    return pl.pallas_call(
        _stream_stats_kernel,
        grid=(t // tm,),
        in_specs=[pl.BlockSpec((tm, d), lambda i: (i, 0))],
        out_specs=[pl.BlockSpec((tm, d), lambda i: (i, 0)), pl.BlockSpec((tm, LANES), lambda i: (i, 0))],
        out_shape=[jax.ShapeDtypeStruct((t, d), BF16), jax.ShapeDtypeStruct((t, LANES), F32)],
        compiler_params=pltpu.CompilerParams(
            dimension_semantics=("arbitrary",),
            vmem_limit_bytes=_vmem_limit([((tm, d), F32), ((tm, d), BF16), ((tm, LANES), F32)])),
        name="stream_stats",
    )(x)


def _mm_body(*refs, nk, n_extra, n_out, epilogue, sub, has_norm, has_gain, stats):
    a_ref, w_ref = refs[0], refs[1]
    pos = 2
    ssq_ref = refs[pos] if has_norm else None
    pos += has_norm
    gain_ref = refs[pos] if has_gain else None
    pos += has_gain
    extra = refs[pos:pos + n_extra]
    outs = refs[pos + n_extra:pos + n_extra + n_out]
    tm = a_ref.shape[0]

    if stats:
        @pl.when((pl.program_id(1) == 0) & (pl.program_id(2) == 0))
        def _():
            outs[-1][...] = jnp.zeros_like(outs[-1])

    def weights():
        w = w_ref[...]
        if has_gain:
            g = gain_ref[...]
            w = jnp.concatenate([w[:, c * LANES:(c + 1) * LANES] * g for c in range(w.shape[1] // LANES)], axis=1)
        return w.astype(BF16)

    def scaled(acc, rows):
        if not has_norm:
            return acc
        ms = jnp.sum(ssq_ref[rows, :], axis=-1, keepdims=True) * (1.0 / D_MODEL)
        return acc * lax.rsqrt(ms + EPS)

    if nk == 1:
        w = weights()
        for r in range(tm // sub):
            rows = slice(r * sub, (r + 1) * sub)
            acc = jnp.dot(a_ref[rows, :], w, preferred_element_type=F32)
            epilogue(scaled(acc, rows), extra, outs, rows)
        return
    assert not has_norm
    acc_ref = refs[-1]
    k = pl.program_id(2)

    def partial_dots():
        w = weights()
        for r in range(tm // sub):
            rows = slice(r * sub, (r + 1) * sub)
            yield rows, jnp.dot(a_ref[rows, :], w, preferred_element_type=F32)

    @pl.when(k == 0)
    def _():
        for rows, d in partial_dots():
            acc_ref[rows, :] = d

    @pl.when((k > 0) & (k < nk - 1))
    def _():
        for rows, d in partial_dots():
            acc_ref[rows, :] += d

    @pl.when(k == nk - 1)
    def _():
        for rows, d in partial_dots():
            epilogue(acc_ref[rows, :] + d, extra, outs, rows)


def _matmul(a, w, *, tm, tn, tk, epilogue, extras=(), outs, name, sub=SUB_ROWS, layer=None, cols=None,
            norm=None, gain=None, stats=False):
    m, kdim = a.shape
    col0, n = cols if cols is not None else (0, w.shape[-1])
    nk = kdim // tk
    assert m % tm == 0 and n % tn == 0 and col0 % tn == 0 and kdim % tk == 0 and tm % min(sub, tm) == 0
    j0 = col0 // tn
    if layer is None:
        w_spec = pl.BlockSpec((tk, tn), lambda i, j, k: (k, j0 + j))
    else:
        w_spec = pl.BlockSpec((None, tk, tn), lambda i, j, k: (layer, k, j0 + j))
    in_specs = [pl.BlockSpec((tm, tk), lambda i, j, k: (i, k)), w_spec]
    blocks = [((tm, tk), a.dtype), ((tk, tn), w.dtype)]
    operands = [a, w]
    if norm is not None:
        extras = [(norm, (tm, LANES), lambda i, j, k: (i, 0))] + list(extras)
    if gain is not None:
        pos = 1 if norm is not None else 0
        extras = list(extras[:pos]) + [(gain, (tk, LANES), lambda i, j, k: (k, 0))] + list(extras[pos:])
    for arr, bshape, imap in extras:
        in_specs.append(pl.BlockSpec(bshape, imap))
        blocks.append((bshape, arr.dtype))
        operands.append(arr)
    out_specs, out_shape = [], []
    for dtype, ocols, bcols in outs:
        out_specs.append(pl.BlockSpec((tm, bcols), lambda i, j, k: (i, j)))
        out_shape.append(jax.ShapeDtypeStruct((m, ocols), dtype))
        blocks.append(((tm, bcols), dtype))
    if stats:
        out_specs.append(pl.BlockSpec((tm, tn), lambda i, j, k: (i, j)))
        out_shape.append(jax.ShapeDtypeStruct((m, n), BF16))
        out_specs.append(pl.BlockSpec((tm, LANES), lambda i, j, k: (i, 0)))
        out_shape.append(jax.ShapeDtypeStruct((m, LANES), F32))
        blocks += [((tm, tn), BF16), ((tm, LANES), F32)]
    scratch = [((tm, tn), F32)] if nk > 1 else []
    n_body = (norm is not None) + (gain is not None)
    return pl.pallas_call(
        functools.partial(_mm_body, nk=nk, n_extra=len(extras) - n_body, n_out=len(out_specs),
                          epilogue=epilogue, sub=min(sub, tm), has_norm=norm is not None,
                          has_gain=gain is not None, stats=stats),
        grid=(m // tm, n // tn, nk),
        in_specs=in_specs,
        out_specs=out_specs,
        out_shape=out_shape,
        scratch_shapes=[pltpu.VMEM(s, d) for s, d in scratch],
        compiler_params=pltpu.CompilerParams(
            dimension_semantics=("arbitrary", "arbitrary", "arbitrary"),
            vmem_limit_bytes=_vmem_limit(blocks, scratch)),
        name=name,
    )(*operands)


def _emit_stream(x_new, outs, rows):
    outs[0][rows, :] = x_new
    if len(outs) == 3:
        outs[1][rows, :] = x_new.astype(BF16)
        sq = x_new * x_new
        part = sq[:, :LANES]
        for c in range(1, sq.shape[1] // LANES):
            part = part + sq[:, c * LANES:(c + 1) * LANES]
        outs[2][rows, :] += part


def _ep_store(acc, extra, outs, rows):
    outs[0][rows, :] = acc.astype(outs[0].dtype)


def _ep_resid(acc, extra, outs, rows):
    _emit_stream(extra[0][rows, :] + acc, outs, rows)


def _ep_relu2(acc, extra, outs, rows):
    r = jnp.maximum(acc, 0.0)
    outs[0][rows, :] = (r * r).astype(outs[0].dtype)


def _ep_gate(acc, extra, outs, rows):
    res_ref, p_ref, wple_ref = extra
    ple = jnp.dot(p_ref[rows, :], wple_ref[...], preferred_element_type=F32)
    gate = jax.nn.sigmoid(acc)
    _emit_stream(res_ref[rows, :] + gate * ple, outs, rows)


def _ep_diff_qk(acc, extra, outs, rows):
    gain_ref, c_ref, s1_ref, s2_ref = extra
    c, s1, s2 = c_ref[rows, :], s1_ref[rows, :], s2_ref[rows, :]
    for g in range(acc.shape[1] // LANES):
        sl = slice(g * LANES, (g + 1) * LANES)
        y = _rms(acc[:, sl], gain_ref[:, sl], DIFF_HEAD_DIM)
        outs[0][rows, sl] = _rope(y, c, s1, s2, DIFF_ROT // 2).astype(outs[0].dtype)


def _ep_mla_in(acc, extra, outs, rows):
    gcq_ref, gckv_ref, gkr_ref, c_ref, s1_ref, s2_ref = extra
    cq_ref, ckv_ref, kr_ref = outs
    cq_ref[rows, :] = _rms(acc[:, :MLA_Q_RANK], gcq_ref[...], MLA_Q_RANK).astype(cq_ref.dtype)
    lo = MLA_Q_RANK
    ckv_ref[rows, :] = _rms(acc[:, lo:lo + MLA_KV_RANK], gckv_ref[...], MLA_KV_RANK).astype(ckv_ref.dtype)
    lo += MLA_KV_RANK
    y = _rms(acc[:, lo:lo + LANES], gkr_ref[...], MLA_ROPE)
    kr_ref[rows, :] = _rope(y, c_ref[rows, :], s1_ref[rows, :], s2_ref[rows, :],
                            MLA_ROPE // 2).astype(kr_ref.dtype)


def _ep_mla_q(acc, extra, outs, rows):
    gn_ref, gr_ref, c_ref, s1_ref, s2_ref = extra
    c, s1, s2 = c_ref[rows, :], s1_ref[rows, :], s2_ref[rows, :]
    for h in range(acc.shape[1] // MLA_QK_PAD):
        lo = h * MLA_QK_PAD
        yn = _rms(acc[:, lo:lo + LANES], gn_ref[...], MLA_NOPE)
        outs[0][rows, lo:lo + LANES] = yn.astype(outs[0].dtype)
        yr = _rms(acc[:, lo + LANES:lo + 2 * LANES], gr_ref[...], MLA_ROPE)
        outs[0][rows, lo + LANES:lo + 2 * LANES] = _rope(yr, c, s1, s2, MLA_ROPE // 2).astype(outs[0].dtype)


def _ep_mla_kv(acc, extra, outs, rows):
    gk_ref, kr_ref, a_ref, wuv_ref = extra
    k_ref, v_ref = outs
    kr = kr_ref[rows, :]
    v = jnp.dot(a_ref[rows, :], wuv_ref[...], preferred_element_type=F32).astype(v_ref.dtype)
    ones = jnp.ones((acc.shape[0], LANES), v_ref.dtype)
    for h in range(acc.shape[1] // LANES):
        yk = _rms(acc[:, h * LANES:(h + 1) * LANES], gk_ref[...], MLA_NOPE)
        lo = h * MLA_QK_PAD
        k_ref[rows, lo:lo + LANES] = yk.astype(k_ref.dtype)
        k_ref[rows, lo + LANES:lo + 2 * LANES] = kr
        lo = h * MLA_V_PAD
        v_ref[rows, lo:lo + LANES] = v[:, h * LANES:(h + 1) * LANES]
        v_ref[rows, lo + LANES:lo + 2 * LANES] = ones


SUBLANE_PARTIALS = 32


def _fold_rows(x, op):
    r = x.shape[0] // SUBLANE_PARTIALS
    return op(x.reshape(r, SUBLANE_PARTIALS, x.shape[1]), axis=0)


def _score_sweep_t(qt, k_ref, k_cols, s_ref, tk):
    mrun = None
    for j in range(s_ref.shape[0]):
        s = jnp.dot(k_ref[j * tk:(j + 1) * tk, k_cols], qt, preferred_element_type=F32)
        s_ref[j] = s
        part = _fold_rows(s, jnp.max)
        mrun = part if mrun is None else jnp.maximum(mrun, part)
    return jnp.max(mrun, axis=0, keepdims=True)


def _diff_scores(qt_ref, t, k_ref, bufs, tk):
    ms = []
    for c, buf in enumerate(bufs):
        cols = slice(c * DIFF_HEAD_DIM, (c + 1) * DIFF_HEAD_DIM)
        ms.append(_score_sweep_t(qt_ref[t, cols, :], k_ref, cols, buf, tk))
    return ms


def _diff_numerators(bufs, ms):
    ls = []
    for buf, m in zip(bufs, ms):
        lrun = None
        for j in range(buf.shape[0]):
            e = jnp.exp2(buf[j] - m)
            buf[j] = e
            part = _fold_rows(e, jnp.sum)
            lrun = part if lrun is None else lrun + part
        ls.append(jnp.sum(lrun, axis=0, keepdims=True))
    return ls


def _diff_substage(qt_ref, t_s, k_ref, bufs_s, bufs_e, ms_e, bufs_c, ls_c, vt_ref, lam_full, tk):
    nchunk = bufs_c[0].shape[0]
    qt_s = [qt_ref[t_s, c * DIFF_HEAD_DIM:(c + 1) * DIFF_HEAD_DIM, :] for c in range(2)]
    rho = lam_full * ls_c[0] / ls_c[1]
    mrun, lrun, acc = [None, None], [None, None], None
    for j in range(nchunk):
        keys = slice(j * tk, (j + 1) * tk)
        for c in range(2):
            cols = slice(c * DIFF_HEAD_DIM, (c + 1) * DIFF_HEAD_DIM)
            s = jnp.dot(k_ref[keys, cols], qt_s[c], preferred_element_type=F32)
            bufs_s[c][j] = s
            part = _fold_rows(s, jnp.max)
            mrun[c] = part if mrun[c] is None else jnp.maximum(mrun[c], part)
        for c in range(2):
            e = jnp.exp2(bufs_e[c][j] - ms_e[c])
            bufs_e[c][j] = e
            part = _fold_rows(e, jnp.sum)
            lrun[c] = part if lrun[c] is None else lrun[c] + part
        a = bufs_c[0][j] - bufs_c[1][j] * rho
        d = jnp.dot(vt_ref[:, keys], a.astype(BF16), preferred_element_type=F32)
        acc = d if acc is None else acc + d
    ms_s = [jnp.max(m, axis=0, keepdims=True) for m in mrun]
    ls_e = [jnp.sum(l, axis=0, keepdims=True) for l in lrun]
    return ms_s, ls_e, acc * (1.0 / ls_c[0])


def _diff_attn_kernel(lam_ref, gsub_ref, q_ref, k_ref, v_ref, o_ref, x0_ref, x1_ref, y0_ref, y1_ref,
                      z0_ref, z1_ref, stat_ref, vt_ref, qt_ref, *, tq, tk, lambda_init):
    lam = lam_ref[...]
    lam_full = (jnp.exp(jnp.sum(lam[0:1] * lam[1:2], axis=-1, keepdims=True))
                - jnp.exp(jnp.sum(lam[2:3] * lam[3:4], axis=-1, keepdims=True)) + lambda_init)
    vt_ref[...] = v_ref[...].T
    ntile = q_ref.shape[0] // tq
    for t in range(ntile):
        qt_ref[t] = q_ref[t * tq:(t + 1) * tq, :].T
    bx, by, bz = (x0_ref, x1_ref), (y0_ref, y1_ref), (z0_ref, z1_ref)

    def rows(t):
        return pl.ds(pl.multiple_of(t * tq, tq), tq)

    def emit(acc, t):
        ms_o = jnp.sum(acc * acc, axis=0, keepdims=True) * (1.0 / (2 * DIFF_HEAD_DIM))
        y = acc * lax.rsqrt(ms_o + EPS) * gsub_ref[...] * (1.0 - lambda_init)
        o_ref[rows(t), :] = y.T.astype(o_ref.dtype)

    ls_x = _diff_numerators(bx, _diff_scores(qt_ref, 0, k_ref, bx, tk))
    ms_y = _diff_scores(qt_ref, 1, k_ref, by, tk)
    for r, v in enumerate(ls_x + ms_y):
        stat_ref[r:r + 1, :] = v

    def triple(i, carry):
        t = 3 * i
        ls_x = [stat_ref[0:1, :], stat_ref[1:2, :]]
        ms_y = [stat_ref[2:3, :], stat_ref[3:4, :]]
        ms_z, ls_y, acc = _diff_substage(qt_ref, t + 2, k_ref, bz, by, ms_y, bx, ls_x, vt_ref, lam_full, tk)
        emit(acc, t)
        ms_x, ls_z, acc = _diff_substage(qt_ref, t + 3, k_ref, bx, bz, ms_z, by, ls_y, vt_ref, lam_full, tk)
        emit(acc, t + 1)
        ms_y, ls_x, acc = _diff_substage(qt_ref, jnp.minimum(t + 4, ntile - 1), k_ref, by, bx, ms_x, bz, ls_z,
                                         vt_ref, lam_full, tk)
        emit(acc, t + 2)
        for r, v in enumerate(ls_x + ms_y):
            stat_ref[r:r + 1, :] = v
        return carry

    lax.fori_loop(0, (ntile - 1) // 3, triple, 0)
    rho = lam_full * stat_ref[0:1, :] / stat_ref[1:2, :]
    acc = None
    for j in range(bx[0].shape[0]):
        a = bx[0][j] - bx[1][j] * rho
        d = jnp.dot(vt_ref[:, j * tk:(j + 1) * tk], a.astype(BF16), preferred_element_type=F32)
        acc = d if acc is None else acc + d
    emit(acc * (1.0 / stat_ref[0:1, :]), ntile - 1)


def _diff_attention(qk, v, lam, g_sub, lambda_init, batch, seq):
    tq, tk = DIFF_TQ, ATT_TK
    hw = 2 * DIFF_HEAD_DIM
    assert (seq // tq - 1) % 3 == 0 and seq // tq >= 4
    blocks = [((seq, hw), BF16)] * 4 + [((hw, tq), F32)]
    scratch = [((seq // tk, tk, tq), F32)] * 6 + [((4, tq), F32), ((hw, seq), BF16), ((seq // tq, hw, tq), BF16)]
    return pl.pallas_call(
        functools.partial(_diff_attn_kernel, tq=tq, tk=tk, lambda_init=lambda_init),
        grid=(batch, DIFF_HEADS),
        in_specs=[pl.BlockSpec((4, DIFF_HEAD_DIM), lambda b, h: (0, 0)),
                  pl.BlockSpec((hw, tq), lambda b, h: (0, 0)),
                  pl.BlockSpec((seq, hw), lambda b, h: (b, h)),
                  pl.BlockSpec((seq, hw), lambda b, h: (b, DIFF_HEADS + h)),
                  pl.BlockSpec((seq, hw), lambda b, h: (b, h))],
        out_specs=pl.BlockSpec((seq, hw), lambda b, h: (b, h)),
        out_shape=jax.ShapeDtypeStruct((batch * seq, DIFF_V_WIDTH), BF16),
        scratch_shapes=[pltpu.VMEM(s, d) for s, d in scratch],
        compiler_params=pltpu.CompilerParams(
            dimension_semantics=("arbitrary", "arbitrary"),
            vmem_limit_bytes=_vmem_limit(blocks, scratch)),
        name="diff_attention",
    )(lam, jnp.broadcast_to(g_sub[:, None], (hw, tq)), qk, qk, v)


def _score_sweep(q, k_ref, s_ref, tk):
    mpart = None
    for j in range(s_ref.shape[0]):
        s = lax.dot_general(q, k_ref[j * tk:(j + 1) * tk, :], (((1,), (1,)), ((), ())),
                            preferred_element_type=F32)
        s_ref[j] = s
        for g in range(tk // LANES):
            sg = s[:, g * LANES:(g + 1) * LANES]
            mpart = sg if mpart is None else jnp.maximum(mpart, sg)
    return jnp.max(mpart, axis=-1, keepdims=True)


def _fused_sweep(q_next, k_ref, s_next_ref, s_ref, m, v_ref, tk):
    mpart, acc = None, None
    for j in range(s_ref.shape[0]):
        s = lax.dot_general(q_next, k_ref[j * tk:(j + 1) * tk, :], (((1,), (1,)), ((), ())),
                            preferred_element_type=F32)
        s_next_ref[j] = s
        e = jnp.exp2(s_ref[j] - m)
        for g in range(tk // LANES):
            sg = s[:, g * LANES:(g + 1) * LANES]
            mpart = sg if mpart is None else jnp.maximum(mpart, sg)
        d = jnp.dot(e.astype(BF16), v_ref[j * tk:(j + 1) * tk, :], preferred_element_type=F32)
        acc = d if acc is None else acc + d
    return jnp.max(mpart, axis=-1, keepdims=True), acc


def _mla_attn_kernel(q_ref, k_ref, v_ref, o_ref, s0_ref, s1_ref, m0_ref, *, tq, tk):
    npair = q_ref.shape[0] // (2 * tq)
    m0_ref[...] = _score_sweep(q_ref[0:tq, :], k_ref, s0_ref, tk)

    def finish(acc, rows):
        o_ref[rows, :] = (acc[:, :MLA_V] / acc[:, MLA_V:MLA_V + 1]).astype(o_ref.dtype)

    def tiles(t):
        return pl.ds(pl.multiple_of(t * tq, tq), tq)

    def pair(i, carry):
        m1, acc = _fused_sweep(q_ref[tiles(2 * i + 1), :], k_ref, s1_ref, s0_ref, m0_ref[...], v_ref, tk)
        finish(acc, tiles(2 * i))
        m0, acc = _fused_sweep(q_ref[tiles(2 * i + 2), :], k_ref, s0_ref, s1_ref, m1, v_ref, tk)
        m0_ref[...] = m0
        finish(acc, tiles(2 * i + 1))
        return carry

    lax.fori_loop(0, npair - 1, pair, 0)
    last = 2 * (npair - 1)
    m1, acc = _fused_sweep(q_ref[(last + 1) * tq:(last + 2) * tq, :], k_ref, s1_ref, s0_ref, m0_ref[...], v_ref, tk)
    finish(acc, slice(last * tq, (last + 1) * tq))
    acc = None
    for j in range(s1_ref.shape[0]):
        e = jnp.exp2(s1_ref[j] - m1)
        d = jnp.dot(e.astype(BF16), v_ref[j * tk:(j + 1) * tk, :], preferred_element_type=F32)
        acc = d if acc is None else acc + d
    finish(acc, slice((last + 1) * tq, (last + 2) * tq))


def _mla_attention(q, k, v, batch, seq):
    tq, tk = ATT_TQ, MLA_TK
    blocks = [((seq, MLA_QK_PAD), BF16)] * 2 + [((seq, MLA_V_PAD), BF16), ((seq, MLA_V), BF16)]
    scratch = [((seq // tk, tq, tk), F32)] * 2 + [((tq, 1), F32)]
    return pl.pallas_call(
        functools.partial(_mla_attn_kernel, tq=tq, tk=tk),
        grid=(batch, MLA_HEADS),
        in_specs=[pl.BlockSpec((seq, MLA_QK_PAD), lambda b, h: (b, h)),
                  pl.BlockSpec((seq, MLA_QK_PAD), lambda b, h: (b, h)),
                  pl.BlockSpec((seq, MLA_V_PAD), lambda b, h: (b, h))],
        out_specs=pl.BlockSpec((seq, MLA_V), lambda b, h: (b, h)),
        out_shape=jax.ShapeDtypeStruct((batch * seq, MLA_HEADS * MLA_V), BF16),
        scratch_shapes=[pltpu.VMEM(s, d) for s, d in scratch],
        compiler_params=pltpu.CompilerParams(
            dimension_semantics=("arbitrary", "arbitrary"),
            vmem_limit_bytes=_vmem_limit(blocks, scratch)),
        name="mla_attention",
    )(q, k, v)


def _row_spec(cols):
    return (TM, cols), (lambda i, j, k: (i, 0))


def _lane_gain(g):
    return jnp.broadcast_to(g[:, None], (g.shape[0], LANES))


def _diff_mixer(stream, g_mix, w_in, w_out, layer, g_q, g_k, lam, g_sub, tabs, lambda_init, batch, seq):
    x, xb, ssq = stream
    qscale = DIFF_HEAD_DIM ** -0.5 * LOG2E
    gain = jnp.concatenate([jnp.tile(g_q * qscale, 2 * DIFF_HEADS), jnp.tile(g_k, 2 * DIFF_HEADS)])[None, :]
    tab_extras = [(t, *_row_spec(LANES)) for t in tabs]
    g_rows = _lane_gain(g_mix)
    (qk,) = _matmul(xb, w_in, layer=layer, cols=(0, 2 * DIFF_QK_WIDTH), tm=TM, tn=TN, tk=D_MODEL,
                    norm=ssq, gain=g_rows, epilogue=_ep_diff_qk,
                    extras=[(gain, (1, TN), lambda i, j, k: (0, j))] + tab_extras,
                    outs=[(BF16, 2 * DIFF_QK_WIDTH, TN)], name="diff_qk_proj")
    (v,) = _matmul(xb, w_in, layer=layer, cols=(2 * DIFF_QK_WIDTH, DIFF_V_WIDTH), tm=TM, tn=TN, tk=D_MODEL,
                   norm=ssq, gain=g_rows, epilogue=_ep_store, outs=[(BF16, DIFF_V_WIDTH, TN)],
                   name="diff_v_proj")
    o = _diff_attention(qk, v, lam, g_sub, lambda_init, batch, seq)
    return _matmul(o, w_out, layer=layer, tm=TM, tn=TN, tk=DIFF_V_WIDTH, epilogue=_ep_resid, stats=True,
                   extras=[(x, (TM, TN), lambda i, j, k: (i, j))],
                   outs=[(F32, D_MODEL, TN)], name="diff_out_proj")


def _mla_mixer(stream, g_mix, w_in, g_cq, g_ckv, w_uq, w_ukv, g_q, g_k, w_out, layer, tabs, batch, seq):
    x, xb, ssq = stream
    tab_extras = [(t, *_row_spec(LANES)) for t in tabs]
    zpad = jnp.zeros((MLA_ROPE,), F32)
    n_in = MLA_Q_RANK + MLA_KV_RANK + LANES
    w_in_p = jnp.pad(w_in * g_mix[:, None], ((0, 0), (0, n_in - w_in.shape[1]))).astype(BF16)
    const = lambda i, j, k: (0, 0)
    tm_in = 512
    cq, ckv, kr = _matmul(
        xb, w_in_p, tm=tm_in, tn=n_in, tk=D_MODEL, norm=ssq, epilogue=_ep_mla_in,
        extras=[(g_cq[None, :], (1, MLA_Q_RANK), const), (g_ckv[None, :], (1, MLA_KV_RANK), const),
                (jnp.concatenate([g_k[MLA_NOPE:], zpad])[None, :], (1, LANES), const)]
               + [(t, (tm_in, LANES), lambda i, j, k: (i, 0)) for t in tabs],
        outs=[(BF16, MLA_Q_RANK, MLA_Q_RANK), (BF16, MLA_KV_RANK, MLA_KV_RANK), (BF16, LANES, LANES)],
        name="mla_in_proj")
    w_uq_h = w_uq.reshape(MLA_Q_RANK, MLA_HEADS, MLA_QK)
    w_uq_p = jnp.pad(w_uq_h, ((0, 0), (0, 0), (0, MLA_QK_PAD - MLA_QK))).reshape(
        MLA_Q_RANK, MLA_HEADS * MLA_QK_PAD).astype(BF16)
    qscale = MLA_QK ** -0.5 * LOG2E
    g_qs = g_q * qscale
    (q,) = _matmul(
        cq, w_uq_p, tm=TM, tn=MLA_TN, tk=MLA_Q_RANK, epilogue=_ep_mla_q,
        extras=[(g_qs[None, :MLA_NOPE], (1, LANES), const),
                (jnp.concatenate([g_qs[MLA_NOPE:], zpad])[None, :], (1, LANES), const)] + tab_extras,
        outs=[(BF16, MLA_HEADS * MLA_QK_PAD, MLA_TN)], name="mla_q_proj")
    w_ukv_h = w_ukv.reshape(MLA_KV_RANK, MLA_HEADS, MLA_NOPE + MLA_V).astype(BF16)
    w_uk = w_ukv_h[:, :, :MLA_NOPE].reshape(MLA_KV_RANK, MLA_HEADS * MLA_NOPE)
    w_uv = w_ukv_h[:, :, MLA_NOPE:].reshape(MLA_KV_RANK, MLA_HEADS * MLA_V)
    k, v = _matmul(
        ckv, w_uk, tm=TM, tn=MLA_TN // 2, tk=MLA_KV_RANK, epilogue=_ep_mla_kv,
        extras=[(g_k[None, :MLA_NOPE], (1, LANES), const),
                (kr, (TM, LANES), lambda i, j, k: (i, 0)),
                (ckv, (TM, MLA_KV_RANK), lambda i, j, k: (i, 0)),
                (w_uv, (MLA_KV_RANK, MLA_TN // 2), lambda i, j, k: (0, j))],
        outs=[(BF16, MLA_HEADS * MLA_QK_PAD, MLA_TN), (BF16, MLA_HEADS * MLA_V_PAD, MLA_TN)],
        name="mla_kv_proj")
    o = _mla_attention(q, k, v, batch, seq)
    return _matmul(o, w_out, layer=layer, tm=TM, tn=TN, tk=MLA_HEADS * MLA_V, epilogue=_ep_resid, stats=True,
                   extras=[(x, (TM, TN), lambda i, j, k: (i, j))],
                   outs=[(F32, D_MODEL, TN)], name="mla_out_proj")


def kernel(x, p, positions, g_mix, g_mlp, g_ple, w1, w2, w_gate, w_ple, diff_w_in, diff_w_out, diff_g_q, diff_g_k, diff_lambda, diff_g_sub, mla_w_in, mla_g_cq, mla_g_ckv, mla_w_uq, mla_w_ukv, mla_g_q, mla_g_k, mla_w_out):
    batch, seq, d = x.shape
    t = batch * seq
    x = x.reshape(t, d)
    pos = positions.reshape(t, 1).astype(F32)
    tabs_d = _rope_tables(pos, DIFF_ROT, keep_rest=True)
    tabs_m = _rope_tables(pos, MLA_ROPE, keep_rest=False)
    p_bf = p.reshape(DEPTH, t, PLE_DIM).astype(BF16)
    stream = (x, *_stream_stats(x))
    for i in range(DEPTH):
        j = i // N_MIXERS
        if i % N_MIXERS == 0:
            lambda_init = 0.8 - 0.6 * math.exp(-0.3 * i)
            stream = _diff_mixer(stream, g_mix[i], diff_w_in, diff_w_out, j, diff_g_q[j], diff_g_k[j],
                                 diff_lambda[j], diff_g_sub[j], tabs_d, lambda_init, batch, seq)
        else:
            stream = _mla_mixer(stream, g_mix[i], mla_w_in[j], mla_g_cq[j], mla_g_ckv[j], mla_w_uq[j],
                                mla_w_ukv[j], mla_g_q[j], mla_g_k[j], mla_w_out, j, tabs_m, batch, seq)
        x, xb, ssq = stream
        (a,) = _matmul(xb, w1, layer=i, tm=TM, tn=TN, tk=D_MODEL, norm=ssq, gain=_lane_gain(g_mlp[i]),
                       epilogue=_ep_relu2, outs=[(BF16, D_FF, TN)], name="mlp_up")
        x, xb, ssq = _matmul(a, w2, layer=i, tm=TM, tn=2 * TN, tk=2048, epilogue=_ep_resid, stats=True,
                             extras=[(x, (TM, 2 * TN), lambda i_, j_, k_: (i_, j_))],
                             outs=[(F32, D_MODEL, 2 * TN)], name="mlp_down")
        stream = _matmul(xb, w_gate, layer=i, tm=TM, tn=TN, tk=D_MODEL, norm=ssq, gain=_lane_gain(g_ple[i]),
                         epilogue=_ep_gate, stats=i + 1 < DEPTH,
                         extras=[(x, (TM, TN), lambda i_, j_, k_: (i_, j_)),
                                 (p_bf[i], (TM, PLE_DIM), lambda i_, j_, k_: (i_, 0)),
                                 (w_ple[i].astype(BF16), (PLE_DIM, TN), lambda i_, j_, k_: (0, j_))],
                         outs=[(F32, D_MODEL, TN)], name="ple_gate")
    return stream[0].reshape(batch, seq, d)
```

```python
import functools
import math

import jax
import jax.numpy as jnp
from jax import lax
from jax.experimental import pallas as pl
from jax.experimental.pallas import tpu as pltpu

D_MODEL = 4096
DEPTH = 4
N_MIXERS = 2
ROPE_THETA = 500000.0
EPS = 1e-6
D_FF = 4 * D_MODEL
PLE_DIM = 256

DIFF_HEAD_DIM = 128
DIFF_HEADS = D_MODEL // (2 * DIFF_HEAD_DIM)
DIFF_ROT = DIFF_HEAD_DIM // 4
DIFF_QK_WIDTH = DIFF_HEADS * 2 * DIFF_HEAD_DIM
DIFF_V_WIDTH = DIFF_HEADS * 2 * DIFF_HEAD_DIM

MLA_HEADS = 32
MLA_NOPE = 128
MLA_ROPE = 64
MLA_V = 128
MLA_QK = MLA_NOPE + MLA_ROPE
MLA_Q_RANK = 1024
MLA_KV_RANK = 512

LANES = 128
MLA_QK_PAD = 2 * LANES
MLA_V_PAD = 2 * LANES
V7X_VMEM_BYTES = 64 * 1024 * 1024
VMEM_TEMP_ALLOWANCE = 12 * 1024 * 1024
LOG2E = math.log2(math.e)

TM = 1024
TN = 512
MLA_TN = 2048
SUB_ROWS = 256
ATT_TQ = 512
DIFF_TQ = 256
ATT_TK = 1024
MLA_TK = 512

F32 = jnp.float32
BF16 = jnp.bfloat16


def _nbytes(shape, dtype):
    return math.prod(shape) * jnp.dtype(dtype).itemsize


def _vmem_limit(blocks, scratch=()):
    need = 2 * sum(_nbytes(s, d) for s, d in blocks) + sum(_nbytes(s, d) for s, d in scratch)
    return min(need + VMEM_TEMP_ALLOWANCE, V7X_VMEM_BYTES - 4 * 1024 * 1024)


def _rope_table_kernel(pos_ref, inv_ref, c_ref, s1_ref, s2_ref, *, half, keep_rest):
    ang = pos_ref[...] * inv_ref[...]
    lane = lax.broadcasted_iota(jnp.int32, ang.shape, 1)
    c = jnp.cos(ang)
    s = jnp.sin(ang)
    rest = 1.0 if keep_rest else 0.0
    c_ref[...] = jnp.where(lane < 2 * half, c, rest)
    s1_ref[...] = jnp.where((lane >= half) & (lane < 2 * half), s, 0.0)
    s2_ref[...] = jnp.where(lane < half, -s, 0.0)


def _rope_tables(pos_f32, rot_dim, keep_rest):
    t = pos_f32.shape[0]
    half = rot_dim // 2
    inv = ROPE_THETA ** (-jnp.arange(0, rot_dim, 2, dtype=F32) / rot_dim)
    inv_row = jnp.concatenate([inv, inv, jnp.zeros((LANES - rot_dim,), F32)])[None, :]
    tm = 1024
    out = jax.ShapeDtypeStruct((t, LANES), F32)
    spec = pl.BlockSpec((tm, LANES), lambda i: (i, 0))
    return pl.pallas_call(
        functools.partial(_rope_table_kernel, half=half, keep_rest=keep_rest),
        grid=(t // tm,),
        in_specs=[pl.BlockSpec((tm, 1), lambda i: (i, 0)),
                  pl.BlockSpec((1, LANES), lambda i: (0, 0))],
        out_specs=[spec, spec, spec],
        out_shape=[out, out, out],
        name="rope_tables",
    )(pos_f32, inv_row)


def _rope(y, c, s1, s2, half):
    return y * c + pltpu.roll(y, half, 1) * s1 + pltpu.roll(y, LANES - half, 1) * s2


def _rms(x, gain, n):
    ms = jnp.sum(x * x, axis=-1, keepdims=True) * (1.0 / n)
    return x * lax.rsqrt(ms + EPS) * gain


def _stream_stats_kernel(x_ref, xb_ref, ssq_ref):
    x = x_ref[...]
    xb_ref[...] = x.astype(BF16)
    sq = x * x
    part = sq[:, :LANES]
    for c in range(1, sq.shape[1] // LANES):
        part = part + sq[:, c * LANES:(c + 1) * LANES]
    ssq_ref[...] = part


def _stream_stats(x):
    t, d = x.shape
    tm = 256
    return pl.pallas_call(
        _stream_stats_kernel,
        grid=(t // tm,),
        in_specs=[pl.BlockSpec((tm, d), lambda i: (i, 0))],
        out_specs=[pl.BlockSpec((tm, d), lambda i: (i, 0)), pl.BlockSpec((tm, LANES), lambda i: (i, 0))],
        out_shape=[jax.ShapeDtypeStruct((t, d), BF16), jax.ShapeDtypeStruct((t, LANES), F32)],
        compiler_params=pltpu.CompilerParams(
            dimension_semantics=("arbitrary",),
            vmem_limit_bytes=_vmem_limit([((tm, d), F32), ((tm, d), BF16), ((tm, LANES), F32)])),
        name="stream_stats",
    )(x)


def _mm_body(*refs, nk, n_extra, n_out, epilogue, sub, has_norm, has_gain, stats):
    a_ref, w_ref = refs[0], refs[1]
    pos = 2
    ssq_ref = refs[pos] if has_norm else None
    pos += has_norm
    gain_ref = refs[pos] if has_gain else None
    pos += has_gain
    extra = refs[pos:pos + n_extra]
    outs = refs[pos + n_extra:pos + n_extra + n_out]
    tm = a_ref.shape[0]

    if stats:
        @pl.when((pl.program_id(1) == 0) & (pl.program_id(2) == 0))
        def _():
            outs[-1][...] = jnp.zeros_like(outs[-1])

    def weights():
        w = w_ref[...]
        if has_gain:
            g = gain_ref[...]
            w = jnp.concatenate([w[:, c * LANES:(c + 1) * LANES] * g for c in range(w.shape[1] // LANES)], axis=1)
        return w.astype(BF16)

    def scaled(acc, rows):
        if not has_norm:
            return acc
        ms = jnp.sum(ssq_ref[rows, :], axis=-1, keepdims=True) * (1.0 / D_MODEL)
        return acc * lax.rsqrt(ms + EPS)

    if nk == 1:
        w = weights()
        for r in range(tm // sub):
            rows = slice(r * sub, (r + 1) * sub)
            acc = jnp.dot(a_ref[rows, :], w, preferred_element_type=F32)
            epilogue(scaled(acc, rows), extra, outs, rows)
        return
    assert not has_norm
    acc_ref = refs[-1]
    k = pl.program_id(2)

    def partial_dots():
        w = weights()
        for r in range(tm // sub):
            rows = slice(r * sub, (r + 1) * sub)
            yield rows, jnp.dot(a_ref[rows, :], w, preferred_element_type=F32)

    @pl.when(k == 0)
    def _():
        for rows, d in partial_dots():
            acc_ref[rows, :] = d

    @pl.when((k > 0) & (k < nk - 1))
    def _():
        for rows, d in partial_dots():
            acc_ref[rows, :] += d

    @pl.when(k == nk - 1)
    def _():
        for rows, d in partial_dots():
            epilogue(acc_ref[rows, :] + d, extra, outs, rows)


def _matmul(a, w, *, tm, tn, tk, epilogue, extras=(), outs, name, sub=SUB_ROWS, layer=None, cols=None,
            norm=None, gain=None, stats=False):
    m, kdim = a.shape
    col0, n = cols if cols is not None else (0, w.shape[-1])
    nk = kdim // tk
    assert m % tm == 0 and n % tn == 0 and col0 % tn == 0 and kdim % tk == 0 and tm % min(sub, tm) == 0
    j0 = col0 // tn
    if layer is None:
        w_spec = pl.BlockSpec((tk, tn), lambda i, j, k: (k, j0 + j))
    else:
        w_spec = pl.BlockSpec((None, tk, tn), lambda i, j, k: (layer, k, j0 + j))
    in_specs = [pl.BlockSpec((tm, tk), lambda i, j, k: (i, k)), w_spec]
    blocks = [((tm, tk), a.dtype), ((tk, tn), w.dtype)]
    operands = [a, w]
    if norm is not None:
        extras = [(norm, (tm, LANES), lambda i, j, k: (i, 0))] + list(extras)
    if gain is not None:
        pos = 1 if norm is not None else 0
        extras = list(extras[:pos]) + [(gain, (tk, LANES), lambda i, j, k: (k, 0))] + list(extras[pos:])
    for arr, bshape, imap in extras:
        single = arr is gain and nk == 1
        in_specs.append(pl.BlockSpec(bshape, imap, pipeline_mode=pl.Buffered(1)) if single
                        else pl.BlockSpec(bshape, imap))
        blocks.append((bshape, arr.dtype))
        operands.append(arr)
    out_specs, out_shape = [], []
    for dtype, ocols, bcols in outs:
        out_specs.append(pl.BlockSpec((tm, bcols), lambda i, j, k: (i, j)))
        out_shape.append(jax.ShapeDtypeStruct((m, ocols), dtype))
        blocks.append(((tm, bcols), dtype))
    if stats:
        out_specs.append(pl.BlockSpec((tm, tn), lambda i, j, k: (i, j)))
        out_shape.append(jax.ShapeDtypeStruct((m, n), BF16))
        out_specs.append(pl.BlockSpec((tm, LANES), lambda i, j, k: (i, 0)))
        out_shape.append(jax.ShapeDtypeStruct((m, LANES), F32))
        blocks += [((tm, tn), BF16), ((tm, LANES), F32)]
    scratch = [((tm, tn), F32)] if nk > 1 else []
    n_body = (norm is not None) + (gain is not None)
    return pl.pallas_call(
        functools.partial(_mm_body, nk=nk, n_extra=len(extras) - n_body, n_out=len(out_specs),
                          epilogue=epilogue, sub=min(sub, tm), has_norm=norm is not None,
                          has_gain=gain is not None, stats=stats),
        grid=(m // tm, n // tn, nk),
        in_specs=in_specs,
        out_specs=out_specs,
        out_shape=out_shape,
        scratch_shapes=[pltpu.VMEM(s, d) for s, d in scratch],
        compiler_params=pltpu.CompilerParams(
            dimension_semantics=("arbitrary", "arbitrary", "arbitrary"),
            vmem_limit_bytes=_vmem_limit(blocks, scratch)),
        name=name,
    )(*operands)


def _emit_stream(x_new, outs, rows):
    outs[0][rows, :] = x_new
    if len(outs) == 3:
        outs[1][rows, :] = x_new.astype(BF16)
        sq = x_new * x_new
        part = sq[:, :LANES]
        for c in range(1, sq.shape[1] // LANES):
            part = part + sq[:, c * LANES:(c + 1) * LANES]
        outs[2][rows, :] += part


def _ep_store(acc, extra, outs, rows):
    outs[0][rows, :] = acc.astype(outs[0].dtype)


def _ep_resid(acc, extra, outs, rows):
    _emit_stream(extra[0][rows, :] + acc, outs, rows)


def _ep_relu2(acc, extra, outs, rows):
    r = jnp.maximum(acc, 0.0)
    outs[0][rows, :] = (r * r).astype(outs[0].dtype)


def _ep_gate(acc, extra, outs, rows):
    res_ref, p_ref, wple_ref = extra
    ple = jnp.dot(p_ref[rows, :], wple_ref[...], preferred_element_type=F32)
    gate = jax.nn.sigmoid(acc)
    _emit_stream(res_ref[rows, :] + gate * ple, outs, rows)


def _ep_diff_qk(acc, extra, outs, rows):
    gain_ref, c_ref, s1_ref, s2_ref = extra
    c, s1, s2 = c_ref[rows, :], s1_ref[rows, :], s2_ref[rows, :]
    for g in range(acc.shape[1] // LANES):
        sl = slice(g * LANES, (g + 1) * LANES)
        y = _rms(acc[:, sl], gain_ref[:, sl], DIFF_HEAD_DIM)
        outs[0][rows, sl] = _rope(y, c, s1, s2, DIFF_ROT // 2).astype(outs[0].dtype)


def _ep_mla_in(acc, extra, outs, rows):
    gcq_ref, gckv_ref, gkr_ref, c_ref, s1_ref, s2_ref = extra
    cq_ref, ckv_ref, kr_ref = outs
    cq_ref[rows, :] = _rms(acc[:, :MLA_Q_RANK], gcq_ref[...], MLA_Q_RANK).astype(cq_ref.dtype)
    lo = MLA_Q_RANK
    ckv_ref[rows, :] = _rms(acc[:, lo:lo + MLA_KV_RANK], gckv_ref[...], MLA_KV_RANK).astype(ckv_ref.dtype)
    lo += MLA_KV_RANK
    y = _rms(acc[:, lo:lo + LANES], gkr_ref[...], MLA_ROPE)
    kr_ref[rows, :] = _rope(y, c_ref[rows, :], s1_ref[rows, :], s2_ref[rows, :],
                            MLA_ROPE // 2).astype(kr_ref.dtype)


def _ep_mla_q(acc, extra, outs, rows):
    gn_ref, gr_ref, c_ref, s1_ref, s2_ref = extra
    c, s1, s2 = c_ref[rows, :], s1_ref[rows, :], s2_ref[rows, :]
    for h in range(acc.shape[1] // MLA_QK_PAD):
        lo = h * MLA_QK_PAD
        yn = _rms(acc[:, lo:lo + LANES], gn_ref[...], MLA_NOPE)
        outs[0][rows, lo:lo + LANES] = yn.astype(outs[0].dtype)
        yr = _rms(acc[:, lo + LANES:lo + 2 * LANES], gr_ref[...], MLA_ROPE)
        outs[0][rows, lo + LANES:lo + 2 * LANES] = _rope(yr, c, s1, s2, MLA_ROPE // 2).astype(outs[0].dtype)


def _ep_mla_kv(acc, extra, outs, rows):
    gk_ref, kr_ref, a_ref, wuv_ref = extra
    k_ref, v_ref = outs
    kr = kr_ref[rows, :]
    v = jnp.dot(a_ref[rows, :], wuv_ref[...], preferred_element_type=F32).astype(v_ref.dtype)
    ones = jnp.ones((acc.shape[0], LANES), v_ref.dtype)
    for h in range(acc.shape[1] // LANES):
        yk = _rms(acc[:, h * LANES:(h + 1) * LANES], gk_ref[...], MLA_NOPE)
        lo = h * MLA_QK_PAD
        k_ref[rows, lo:lo + LANES] = yk.astype(k_ref.dtype)
        k_ref[rows, lo + LANES:lo + 2 * LANES] = kr
        lo = h * MLA_V_PAD
        v_ref[rows, lo:lo + LANES] = v[:, h * LANES:(h + 1) * LANES]
        v_ref[rows, lo + LANES:lo + 2 * LANES] = ones


SUBLANE_PARTIALS = 32


def _fold_rows(x, op):
    r = x.shape[0] // SUBLANE_PARTIALS
    return op(x.reshape(r, SUBLANE_PARTIALS, x.shape[1]), axis=0)


def _score_sweep_t(qt, k_ref, k_cols, s_ref, tk):
    mrun = None
    for j in range(s_ref.shape[0]):
        s = jnp.dot(k_ref[j * tk:(j + 1) * tk, k_cols], qt, preferred_element_type=F32)
        s_ref[j] = s
        part = _fold_rows(s, jnp.max)
        mrun = part if mrun is None else jnp.maximum(mrun, part)
    return jnp.max(mrun, axis=0, keepdims=True)


def _diff_scores(qt_ref, t, k_ref, bufs, tk):
    ms = []
    for c, buf in enumerate(bufs):
        cols = slice(c * DIFF_HEAD_DIM, (c + 1) * DIFF_HEAD_DIM)
        ms.append(_score_sweep_t(qt_ref[t, cols, :], k_ref, cols, buf, tk))
    return ms


def _diff_numerators(bufs, ms):
    ls = []
    for buf, m in zip(bufs, ms):
        lrun = None
        for j in range(buf.shape[0]):
            e = jnp.exp2(buf[j] - m)
            buf[j] = e
            part = _fold_rows(e, jnp.sum)
            lrun = part if lrun is None else lrun + part
        ls.append(jnp.sum(lrun, axis=0, keepdims=True))
    return ls


def _diff_substage(qt_ref, t_s, k_ref, bufs_s, bufs_e, ms_e, bufs_c, ls_c, vt_ref, lam_full, tk):
    nchunk = bufs_c[0].shape[0]
    qt_s = [qt_ref[t_s, c * DIFF_HEAD_DIM:(c + 1) * DIFF_HEAD_DIM, :] for c in range(2)]
    rho = lam_full * ls_c[0] / ls_c[1]
    mrun, lrun, acc = [None, None], [None, None], None
    for j in range(nchunk):
        keys = slice(j * tk, (j + 1) * tk)
        for c in range(2):
            cols = slice(c * DIFF_HEAD_DIM, (c + 1) * DIFF_HEAD_DIM)
            s = jnp.dot(k_ref[keys, cols], qt_s[c], preferred_element_type=F32)
            bufs_s[c][j] = s
            part = _fold_rows(s, jnp.max)
            mrun[c] = part if mrun[c] is None else jnp.maximum(mrun[c], part)
        for c in range(2):
            e = jnp.exp2(bufs_e[c][j] - ms_e[c])
            bufs_e[c][j] = e
            part = _fold_rows(e, jnp.sum)
            lrun[c] = part if lrun[c] is None else lrun[c] + part
        a = bufs_c[0][j] - bufs_c[1][j] * rho
        d = jnp.dot(vt_ref[:, keys], a.astype(BF16), preferred_element_type=F32)
        acc = d if acc is None else acc + d
    ms_s = [jnp.max(m, axis=0, keepdims=True) for m in mrun]
    ls_e = [jnp.sum(l, axis=0, keepdims=True) for l in lrun]
    return ms_s, ls_e, acc * (1.0 / ls_c[0])


def _diff_attn_kernel(lam_ref, gsub_ref, q_ref, k_ref, v_ref, o_ref, x0_ref, x1_ref, y0_ref, y1_ref,
                      z0_ref, z1_ref, stat_ref, vt_ref, qt_ref, *, tq, tk, lambda_init):
    lam = lam_ref[...]
    lam_full = (jnp.exp(jnp.sum(lam[0:1] * lam[1:2], axis=-1, keepdims=True))
                - jnp.exp(jnp.sum(lam[2:3] * lam[3:4], axis=-1, keepdims=True)) + lambda_init)
    vt_ref[...] = v_ref[...].T
    ntile = q_ref.shape[0] // tq
    for t in range(ntile):
        qt_ref[t] = q_ref[t * tq:(t + 1) * tq, :].T
    bx, by, bz = (x0_ref, x1_ref), (y0_ref, y1_ref), (z0_ref, z1_ref)

    def rows(t):
        return pl.ds(pl.multiple_of(t * tq, tq), tq)

    def emit(acc, t):
        ms_o = jnp.sum(acc * acc, axis=0, keepdims=True) * (1.0 / (2 * DIFF_HEAD_DIM))
        y = acc * lax.rsqrt(ms_o + EPS) * gsub_ref[...] * (1.0 - lambda_init)
        o_ref[rows(t), :] = y.T.astype(o_ref.dtype)

    ls_x = _diff_numerators(bx, _diff_scores(qt_ref, 0, k_ref, bx, tk))
    ms_y = _diff_scores(qt_ref, 1, k_ref, by, tk)
    for r, v in enumerate(ls_x + ms_y):
        stat_ref[r:r + 1, :] = v

    def triple(i, carry):
        t = 3 * i
        ls_x = [stat_ref[0:1, :], stat_ref[1:2, :]]
        ms_y = [stat_ref[2:3, :], stat_ref[3:4, :]]
        ms_z, ls_y, acc = _diff_substage(qt_ref, t + 2, k_ref, bz, by, ms_y, bx, ls_x, vt_ref, lam_full, tk)
        emit(acc, t)
        ms_x, ls_z, acc = _diff_substage(qt_ref, t + 3, k_ref, bx, bz, ms_z, by, ls_y, vt_ref, lam_full, tk)
        emit(acc, t + 1)
        ms_y, ls_x, acc = _diff_substage(qt_ref, jnp.minimum(t + 4, ntile - 1), k_ref, by, bx, ms_x, bz, ls_z,
                                         vt_ref, lam_full, tk)
        emit(acc, t + 2)
        for r, v in enumerate(ls_x + ms_y):
            stat_ref[r:r + 1, :] = v
        return carry

    lax.fori_loop(0, (ntile - 1) // 3, triple, 0)
    rho = lam_full * stat_ref[0:1, :] / stat_ref[1:2, :]
    acc = None
    for j in range(bx[0].shape[0]):
        a = bx[0][j] - bx[1][j] * rho
        d = jnp.dot(vt_ref[:, j * tk:(j + 1) * tk], a.astype(BF16), preferred_element_type=F32)
        acc = d if acc is None else acc + d
    emit(acc * (1.0 / stat_ref[0:1, :]), ntile - 1)


def _diff_attention(qk, v, lam, g_sub, lambda_init, batch, seq):
    tq, tk = DIFF_TQ, ATT_TK
    hw = 2 * DIFF_HEAD_DIM
    assert (seq // tq - 1) % 3 == 0 and seq // tq >= 4
    blocks = [((seq, hw), BF16)] * 4 + [((hw, tq), F32)]
    scratch = [((seq // tk, tk, tq), F32)] * 6 + [((4, tq), F32), ((hw, seq), BF16), ((seq // tq, hw, tq), BF16)]
    return pl.pallas_call(
        functools.partial(_diff_attn_kernel, tq=tq, tk=tk, lambda_init=lambda_init),
        grid=(batch, DIFF_HEADS),
        in_specs=[pl.BlockSpec((4, DIFF_HEAD_DIM), lambda b, h: (0, 0)),
                  pl.BlockSpec((hw, tq), lambda b, h: (0, 0)),
                  pl.BlockSpec((seq, hw), lambda b, h: (b, h)),
                  pl.BlockSpec((seq, hw), lambda b, h: (b, DIFF_HEADS + h)),
                  pl.BlockSpec((seq, hw), lambda b, h: (b, h))],
        out_specs=pl.BlockSpec((seq, hw), lambda b, h: (b, h)),
        out_shape=jax.ShapeDtypeStruct((batch * seq, DIFF_V_WIDTH), BF16),
        scratch_shapes=[pltpu.VMEM(s, d) for s, d in scratch],
        compiler_params=pltpu.CompilerParams(
            dimension_semantics=("arbitrary", "arbitrary"),
            vmem_limit_bytes=_vmem_limit(blocks, scratch)),
        name="diff_attention",
    )(lam, jnp.broadcast_to(g_sub[:, None], (hw, tq)), qk, qk, v)


def _score_sweep(q, k_ref, s_ref, tk):
    mpart = None
    for j in range(s_ref.shape[0]):
        s = lax.dot_general(q, k_ref[j * tk:(j + 1) * tk, :], (((1,), (1,)), ((), ())),
                            preferred_element_type=F32)
        s_ref[j] = s
        for g in range(tk // LANES):
            sg = s[:, g * LANES:(g + 1) * LANES]
            mpart = sg if mpart is None else jnp.maximum(mpart, sg)
    return jnp.max(mpart, axis=-1, keepdims=True)


def _fused_sweep(q_next, k_ref, s_next_ref, s_ref, m, v_ref, tk):
    mpart, acc = None, None
    for j in range(s_ref.shape[0]):
        s = lax.dot_general(q_next, k_ref[j * tk:(j + 1) * tk, :], (((1,), (1,)), ((), ())),
                            preferred_element_type=F32)
        s_next_ref[j] = s
        e = jnp.exp2(s_ref[j] - m)
        for g in range(tk // LANES):
            sg = s[:, g * LANES:(g + 1) * LANES]
            mpart = sg if mpart is None else jnp.maximum(mpart, sg)
        d = jnp.dot(e.astype(BF16), v_ref[j * tk:(j + 1) * tk, :], preferred_element_type=F32)
        acc = d if acc is None else acc + d
    return jnp.max(mpart, axis=-1, keepdims=True), acc


def _mla_attn_kernel(q_ref, k_ref, v_ref, o_ref, s0_ref, s1_ref, m0_ref, *, tq, tk):
    npair = q_ref.shape[0] // (2 * tq)
    m0_ref[...] = _score_sweep(q_ref[0:tq, :], k_ref, s0_ref, tk)

    def finish(acc, rows):
        o_ref[rows, :] = (acc[:, :MLA_V] / acc[:, MLA_V:MLA_V + 1]).astype(o_ref.dtype)

    def tiles(t):
        return pl.ds(pl.multiple_of(t * tq, tq), tq)

    def pair(i, carry):
        m1, acc = _fused_sweep(q_ref[tiles(2 * i + 1), :], k_ref, s1_ref, s0_ref, m0_ref[...], v_ref, tk)
        finish(acc, tiles(2 * i))
        m0, acc = _fused_sweep(q_ref[tiles(2 * i + 2), :], k_ref, s0_ref, s1_ref, m1, v_ref, tk)
        m0_ref[...] = m0
        finish(acc, tiles(2 * i + 1))
        return carry

    lax.fori_loop(0, npair - 1, pair, 0)
    last = 2 * (npair - 1)
    m1, acc = _fused_sweep(q_ref[(last + 1) * tq:(last + 2) * tq, :], k_ref, s1_ref, s0_ref, m0_ref[...], v_ref, tk)
    finish(acc, slice(last * tq, (last + 1) * tq))
    acc = None
    for j in range(s1_ref.shape[0]):
        e = jnp.exp2(s1_ref[j] - m1)
        d = jnp.dot(e.astype(BF16), v_ref[j * tk:(j + 1) * tk, :], preferred_element_type=F32)
        acc = d if acc is None else acc + d
    finish(acc, slice((last + 1) * tq, (last + 2) * tq))


def _mla_attention(q, k, v, batch, seq):
    tq, tk = ATT_TQ, MLA_TK
    blocks = [((seq, MLA_QK_PAD), BF16)] * 2 + [((seq, MLA_V_PAD), BF16), ((seq, MLA_V), BF16)]
    scratch = [((seq // tk, tq, tk), F32)] * 2 + [((tq, 1), F32)]
    return pl.pallas_call(
        functools.partial(_mla_attn_kernel, tq=tq, tk=tk),
        grid=(batch, MLA_HEADS),
        in_specs=[pl.BlockSpec((seq, MLA_QK_PAD), lambda b, h: (b, h)),
                  pl.BlockSpec((seq, MLA_QK_PAD), lambda b, h: (b, h)),
                  pl.BlockSpec((seq, MLA_V_PAD), lambda b, h: (b, h))],
        out_specs=pl.BlockSpec((seq, MLA_V), lambda b, h: (b, h)),
        out_shape=jax.ShapeDtypeStruct((batch * seq, MLA_HEADS * MLA_V), BF16),
        scratch_shapes=[pltpu.VMEM(s, d) for s, d in scratch],
        compiler_params=pltpu.CompilerParams(
            dimension_semantics=("arbitrary", "arbitrary"),
            vmem_limit_bytes=_vmem_limit(blocks, scratch)),
        name="mla_attention",
    )(q, k, v)


def _row_spec(cols):
    return (TM, cols), (lambda i, j, k: (i, 0))


def _lane_gain(g):
    return jnp.broadcast_to(g[:, None], (g.shape[0], LANES))


def _diff_mixer(stream, g_mix, w_in, w_out, layer, g_q, g_k, lam, g_sub, tabs, lambda_init, batch, seq):
    x, xb, ssq = stream
    qscale = DIFF_HEAD_DIM ** -0.5 * LOG2E
    gain = jnp.concatenate([jnp.tile(g_q * qscale, 2 * DIFF_HEADS), jnp.tile(g_k, 2 * DIFF_HEADS)])[None, :]
    tab_extras = [(t, *_row_spec(LANES)) for t in tabs]
    g_rows = _lane_gain(g_mix)
    (qk,) = _matmul(xb, w_in, layer=layer, cols=(0, 2 * DIFF_QK_WIDTH), tm=TM, tn=TN, tk=D_MODEL,
                    norm=ssq, gain=g_rows, epilogue=_ep_diff_qk,
                    extras=[(gain, (1, TN), lambda i, j, k: (0, j))] + tab_extras,
                    outs=[(BF16, 2 * DIFF_QK_WIDTH, TN)], name="diff_qk_proj")
    (v,) = _matmul(xb, w_in, layer=layer, cols=(2 * DIFF_QK_WIDTH, DIFF_V_WIDTH), tm=TM, tn=TN, tk=D_MODEL,
                   norm=ssq, gain=g_rows, epilogue=_ep_store, outs=[(BF16, DIFF_V_WIDTH, TN)],
                   name="diff_v_proj")
    o = _diff_attention(qk, v, lam, g_sub, lambda_init, batch, seq)
    return _matmul(o, w_out, layer=layer, tm=TM, tn=TN, tk=DIFF_V_WIDTH, epilogue=_ep_resid, stats=True,
                   extras=[(x, (TM, TN), lambda i, j, k: (i, j))],
                   outs=[(F32, D_MODEL, TN)], name="diff_out_proj")


def _mla_mixer(stream, g_mix, w_in, g_cq, g_ckv, w_uq, w_ukv, g_q, g_k, w_out, layer, tabs, batch, seq):
    x, xb, ssq = stream
    tab_extras = [(t, *_row_spec(LANES)) for t in tabs]
    zpad = jnp.zeros((MLA_ROPE,), F32)
    n_in = MLA_Q_RANK + MLA_KV_RANK + LANES
    w_in_p = jnp.pad(w_in * g_mix[:, None], ((0, 0), (0, n_in - w_in.shape[1]))).astype(BF16)
    const = lambda i, j, k: (0, 0)
    tm_in = 512
    cq, ckv, kr = _matmul(
        xb, w_in_p, tm=tm_in, tn=n_in, tk=D_MODEL, norm=ssq, epilogue=_ep_mla_in,
        extras=[(g_cq[None, :], (1, MLA_Q_RANK), const), (g_ckv[None, :], (1, MLA_KV_RANK), const),
                (jnp.concatenate([g_k[MLA_NOPE:], zpad])[None, :], (1, LANES), const)]
               + [(t, (tm_in, LANES), lambda i, j, k: (i, 0)) for t in tabs],
        outs=[(BF16, MLA_Q_RANK, MLA_Q_RANK), (BF16, MLA_KV_RANK, MLA_KV_RANK), (BF16, LANES, LANES)],
        name="mla_in_proj")
    w_uq_h = w_uq.reshape(MLA_Q_RANK, MLA_HEADS, MLA_QK)
    w_uq_p = jnp.pad(w_uq_h, ((0, 0), (0, 0), (0, MLA_QK_PAD - MLA_QK))).reshape(
        MLA_Q_RANK, MLA_HEADS * MLA_QK_PAD).astype(BF16)
    qscale = MLA_QK ** -0.5 * LOG2E
    g_qs = g_q * qscale
    (q,) = _matmul(
        cq, w_uq_p, tm=TM, tn=MLA_TN, tk=MLA_Q_RANK, epilogue=_ep_mla_q,
        extras=[(g_qs[None, :MLA_NOPE], (1, LANES), const),
                (jnp.concatenate([g_qs[MLA_NOPE:], zpad])[None, :], (1, LANES), const)] + tab_extras,
        outs=[(BF16, MLA_HEADS * MLA_QK_PAD, MLA_TN)], name="mla_q_proj")
    w_ukv_h = w_ukv.reshape(MLA_KV_RANK, MLA_HEADS, MLA_NOPE + MLA_V).astype(BF16)
    w_uk = w_ukv_h[:, :, :MLA_NOPE].reshape(MLA_KV_RANK, MLA_HEADS * MLA_NOPE)
    w_uv = w_ukv_h[:, :, MLA_NOPE:].reshape(MLA_KV_RANK, MLA_HEADS * MLA_V)
    k, v = _matmul(
        ckv, w_uk, tm=TM, tn=MLA_TN // 2, tk=MLA_KV_RANK, epilogue=_ep_mla_kv,
        extras=[(g_k[None, :MLA_NOPE], (1, LANES), const),
                (kr, (TM, LANES), lambda i, j, k: (i, 0)),
                (ckv, (TM, MLA_KV_RANK), lambda i, j, k: (i, 0)),
                (w_uv, (MLA_KV_RANK, MLA_TN // 2), lambda i, j, k: (0, j))],
        outs=[(BF16, MLA_HEADS * MLA_QK_PAD, MLA_TN), (BF16, MLA_HEADS * MLA_V_PAD, MLA_TN)],
        name="mla_kv_proj")
    o = _mla_attention(q, k, v, batch, seq)
    return _matmul(o, w_out, layer=layer, tm=TM, tn=TN, tk=MLA_HEADS * MLA_V, epilogue=_ep_resid, stats=True,
                   extras=[(x, (TM, TN), lambda i, j, k: (i, j))],
                   outs=[(F32, D_MODEL, TN)], name="mla_out_proj")


def kernel(x, p, positions, g_mix, g_mlp, g_ple, w1, w2, w_gate, w_ple, diff_w_in, diff_w_out, diff_g_q, diff_g_k, diff_lambda, diff_g_sub, mla_w_in, mla_g_cq, mla_g_ckv, mla_w_uq, mla_w_ukv, mla_g_q, mla_g_k, mla_w_out):
    batch, seq, d = x.shape
    t = batch * seq
    x = x.reshape(t, d)
    pos = positions.reshape(t, 1).astype(F32)
    tabs_d = _rope_tables(pos, DIFF_ROT, keep_rest=True)
    tabs_m = _rope_tables(pos, MLA_ROPE, keep_rest=False)
    p_bf = p.reshape(DEPTH, t, PLE_DIM).astype(BF16)
    stream = (x, *_stream_stats(x))
    for i in range(DEPTH):
        j = i // N_MIXERS
        if i % N_MIXERS == 0:
            lambda_init = 0.8 - 0.6 * math.exp(-0.3 * i)
            stream = _diff_mixer(stream, g_mix[i], diff_w_in, diff_w_out, j, diff_g_q[j], diff_g_k[j],
                                 diff_lambda[j], diff_g_sub[j], tabs_d, lambda_init, batch, seq)
        else:
            stream = _mla_mixer(stream, g_mix[i], mla_w_in[j], mla_g_cq[j], mla_g_ckv[j], mla_w_uq[j],
                                mla_w_ukv[j], mla_g_q[j], mla_g_k[j], mla_w_out, j, tabs_m, batch, seq)
        x, xb, ssq = stream
        (a,) = _matmul(xb, w1, layer=i, tm=TM, tn=TN, tk=D_MODEL, norm=ssq, gain=_lane_gain(g_mlp[i]),
                       epilogue=_ep_relu2, outs=[(BF16, D_FF, TN)], name="mlp_up")
        x, xb, ssq = _matmul(a, w2, layer=i, tm=TM, tn=2 * TN, tk=2048, epilogue=_ep_resid, stats=True,
                             extras=[(x, (TM, 2 * TN), lambda i_, j_, k_: (i_, j_))],
                             outs=[(F32, D_MODEL, 2 * TN)], name="mlp_down")
        stream = _matmul(xb, w_gate, layer=i, tm=TM, tn=TN, tk=D_MODEL, norm=ssq, gain=_lane_gain(g_ple[i]),
                         epilogue=_ep_gate, stats=i + 1 < DEPTH,
                         extras=[(x, (TM, TN), lambda i_, j_, k_: (i_, j_)),
                                 (p_bf[i], (TM, PLE_DIM), lambda i_, j_, k_: (i_, 0)),
                                 (w_ple[i].astype(BF16), (PLE_DIM, TN), lambda i_, j_, k_: (0, j_))],
                         outs=[(F32, D_MODEL, TN)], name="ple_gate")
    return stream[0].reshape(batch, seq, d)
```
